```python
import math
import jax, jax.numpy as jnp
from jax import lax
import numpy as np

D_MODEL = 1024
BATCH = 4
SEQ = 8192
DEPTH = 2

HEAD_DIM = 64
Q_BLOCK = 128
A_HEADS = 4
A_QK = A_HEADS * 2 * HEAD_DIM
A_V = A_HEADS * 2 * HEAD_DIM
B_HEADS = 8
B_W = B_HEADS * HEAD_DIM
GRID_W = 64
NA_ROWS_MAX = 8
NA_COLS = 16
C_HEADS = 8
C_W = C_HEADS * HEAD_DIM
C_CONFIGS = ((128, 1), (512, 4), (2048, 16))
T5_BUCKETS = 32
T5_MAX_DIST = 1024
T5_HEADS = A_HEADS + C_HEADS
D_FF = 2816
N_BRANCH = 3
IN_SPLITS = [A_QK, A_QK, A_V, B_W, B_W, B_W, C_W, C_W, C_W]
IN_COLS = sum(IN_SPLITS) + N_BRANCH * D_MODEL
EPS = 1e-6

kernel_name = "hybrid_gated_diff_na_dilated_encoder"


def rmsnorm(x, g):
    xf = x.astype(jnp.float32)
    y = xf * lax.rsqrt(jnp.mean(xf * xf, axis=-1, keepdims=True) + EPS)
    return (y * g.astype(jnp.float32)).astype(x.dtype)


def swiglu(h, w1, w3, w2):
    return (jax.nn.silu(h @ w1) * (h @ w3)) @ w2


def t5_bucket(rel):
    nb = T5_BUCKETS // 2
    max_exact = nb // 2
    ret = jnp.where(rel > 0, nb, 0)
    n = jnp.abs(rel)
    nf = jnp.maximum(n, 1).astype(jnp.float32)
    large = max_exact + (jnp.log(nf / max_exact) / math.log(T5_MAX_DIST / max_exact)
                         * (nb - max_exact)).astype(jnp.int32)
    large = jnp.minimum(large, nb - 1)
    return ret + jnp.where(n < max_exact, n, large)


def diff_attention(q, k, v, bias_table, lam_full, subln_g, lam_init):
    B, S = q.shape[0], q.shape[1]
    nblk = S // Q_BLOCK
    scale = HEAD_DIM ** -0.5
    qb = (q * scale).reshape(B, nblk, Q_BLOCK, A_HEADS, 2, HEAD_DIM).transpose(1, 0, 3, 4, 2, 5)
    kt = k.transpose(0, 2, 3, 1, 4)
    vt = v.transpose(0, 2, 1, 3)
    kpos = jnp.arange(S)

    def block(args):
        qblk, i = args
        qpos = i * Q_BLOCK + jnp.arange(Q_BLOCK)
        bias = bias_table[t5_bucket(kpos[None, :] - qpos[:, None])].astype(jnp.float32)
        s = jnp.einsum('bhmqd,bhmkd->bhmqk', qblk, kt).astype(jnp.float32)
        s = s + bias.transpose(2, 0, 1)[None, :, None]
        pr = jax.nn.softmax(s, axis=-1)
        attn = pr[:, :, 0] - lam_full.astype(jnp.float32) * pr[:, :, 1]
        return jnp.einsum('bhqk,bhke->bhqe', attn.astype(v.dtype), vt)

    o = lax.map(block, (qb, jnp.arange(nblk)))
    o = o.transpose(1, 0, 3, 2, 4).reshape(B, S, A_HEADS, 2 * HEAD_DIM)
    o = rmsnorm(o, subln_g) * (1.0 - lam_init)
    return o.reshape(B, S, A_V)


def neighborhood_attention(q, k, v, rpb):
    B, S = q.shape[0], q.shape[1]
    rows = S // GRID_W
    kh = min(NA_ROWS_MAX, rows)
    kw = min(NA_COLS, GRID_W)
    scale = HEAD_DIM ** -0.5
    grid = lambda t: t.reshape(B, rows, GRID_W, B_HEADS, HEAD_DIM)
    qg = (grid(q) * scale).transpose(1, 0, 3, 2, 4)
    kg = grid(k).transpose(0, 3, 1, 2, 4)
    vg = grid(v).transpose(0, 3, 1, 2, 4)
    col = jnp.arange(GRID_W)
    cstart = jnp.clip(col - kw // 2, 0, GRID_W - kw)
    cidx = cstart[:, None] + jnp.arange(kw)[None, :]
    rel_c = cidx - col[:, None] + (NA_COLS - 1)

    def row_block(args):
        qrow, i = args
        rstart = jnp.clip(i - kh // 2, 0, rows - kh)
        krow = lax.dynamic_slice_in_dim(kg, rstart, kh, axis=2)
        vrow = lax.dynamic_slice_in_dim(vg, rstart, kh, axis=2)
        knb = krow[:, :, :, cidx]
        vnb = vrow[:, :, :, cidx]
        rel_r = rstart + jnp.arange(kh) - i + (NA_ROWS_MAX - 1)
        bias = rpb[:, rel_r[:, None, None], rel_c[None]].astype(jnp.float32)
        s = jnp.einsum('bhqd,bhrqcd->bhqrc', qrow, knb).astype(jnp.float32)
        s = s + bias.transpose(0, 2, 1, 3)[None]
        pr = jax.nn.softmax(s.reshape(B, B_HEADS, GRID_W, kh * kw), axis=-1)
        pr = pr.reshape(B, B_HEADS, GRID_W, kh, kw)
        return jnp.einsum('bhqrc,bhrqcd->bhqd', pr.astype(v.dtype), vnb)

    o = lax.map(row_block, (qg, jnp.arange(rows)))
    return o.transpose(1, 0, 3, 2, 4).reshape(B, S, B_W)


def dilated_attention(q, k, v, bias_table):
    B, S = q.shape[0], q.shape[1]
    nblk = S // Q_BLOCK
    scale = HEAD_DIM ** -0.5
    qb = (q * scale).reshape(B, nblk, Q_BLOCK, C_HEADS, HEAD_DIM).transpose(1, 0, 3, 2, 4)
    kt = k.transpose(0, 2, 1, 3)
    vt = v.transpose(0, 2, 1, 3)
    cfgs = [(w // (2 * r), r) for (w, r) in C_CONFIGS]
    max_pad = max(hf * r for (hf, r) in cfgs)
    kp = jnp.pad(kt, ((0, 0), (0, 0), (max_pad, max_pad), (0, 0)))
    vp = jnp.pad(vt, ((0, 0), (0, 0), (max_pad, max_pad), (0, 0)))

    def block(args):
        qblk, i = args
        qpos = i * Q_BLOCK + jnp.arange(Q_BLOCK)
        outs, lses = [], []
        for half, r in cfgs:
            offs = r * jnp.arange(-half, half + 1)
            kpos = qpos[:, None] + offs[None, :]
            valid = (kpos >= 0) & (kpos < S)
            kgat = kp[:, :, kpos + max_pad]
            vgat = vp[:, :, kpos + max_pad]
            bias = bias_table[t5_bucket(offs)].astype(jnp.float32)
            s = jnp.einsum('bhqd,bhqjd->bhqj', qblk, kgat).astype(jnp.float32)
            s = jnp.where(valid[None, None], s + bias.T[None, :, None, :], -jnp.inf)
            m = jnp.max(s, axis=-1, keepdims=True)
            e = jnp.exp(s - m)
            den = jnp.sum(e, axis=-1, keepdims=True)
            outs.append(jnp.einsum('bhqj,bhqjd->bhqd', (e / den).astype(v.dtype), vgat))
            lses.append(m + jnp.log(den))
        wts = jax.nn.softmax(jnp.concatenate(lses, axis=-1), axis=-1)
        return jnp.einsum('bhqn,nbhqd->bhqd', wts.astype(v.dtype), jnp.stack(outs))

    o = lax.map(block, (qb, jnp.arange(nblk)))
    return o.transpose(1, 0, 3, 2, 4).reshape(B, S, C_W)


def setup_inputs(seed: int = 0) -> dict:
    key = jax.random.key(seed)
    ks = jax.random.split(key, 26)
    nrm = lambda k, shape, s: jax.random.normal(k, shape, jnp.float32) * s
    gain = lambda k, shape: 1.0 + 0.05 * jax.random.normal(k, shape, jnp.float32)
    D = D_MODEL
    return {
        "x": nrm(ks[0], (BATCH, SEQ, D), 1.0),
        "g_ff1": gain(ks[1], (DEPTH, D)),
        "w1_ff1": nrm(ks[2], (DEPTH, D, D_FF), D ** -0.5),
        "w3_ff1": nrm(ks[3], (DEPTH, D, D_FF), D ** -0.5),
        "w2_ff1": nrm(ks[4], (DEPTH, D_FF, D), D_FF ** -0.5),
        "g_mix": gain(ks[5], (DEPTH, D)),
        "w_in": nrm(ks[6], (DEPTH, D, IN_COLS), D ** -0.5),
        "lam_q1": nrm(ks[7], (DEPTH, HEAD_DIM), 0.1),
        "lam_k1": nrm(ks[8], (DEPTH, HEAD_DIM), 0.1),
        "lam_q2": nrm(ks[9], (DEPTH, HEAD_DIM), 0.1),
        "lam_k2": nrm(ks[10], (DEPTH, HEAD_DIM), 0.1),
        "subln_g": gain(ks[11], (DEPTH, 2 * HEAD_DIM)),
        "na_rpb": nrm(ks[12], (DEPTH, B_HEADS, 2 * NA_ROWS_MAX - 1, 2 * NA_COLS - 1), 0.5),
        "t5_table": nrm(ks[13], (T5_BUCKETS, T5_HEADS), 0.5),
        "w_br_a": nrm(ks[14], (DEPTH, A_V, D), A_V ** -0.5),
        "w_br_b": nrm(ks[15], (DEPTH, B_W, D), B_W ** -0.5),
        "w_br_c": nrm(ks[16], (DEPTH, C_W, D), C_W ** -0.5),
        "w_o": nrm(ks[17], (DEPTH, D, D), D ** -0.5),
        "g_ff2": gain(ks[18], (DEPTH, D)),
        "w1_ff2": nrm(ks[19], (DEPTH, D, D_FF), D ** -0.5),
        "w3_ff2": nrm(ks[20], (DEPTH, D, D_FF), D ** -0.5),
        "w2_ff2": nrm(ks[21], (DEPTH, D_FF, D), D_FF ** -0.5),
        "g_final": gain(ks[22], (D,)),
    }


def reference(x, g_ff1, w1_ff1, w3_ff1, w2_ff1, g_mix, w_in, lam_q1, lam_k1, lam_q2, lam_k2,
              subln_g, na_rpb, t5_table, w_br_a, w_br_b, w_br_c, w_o, g_ff2, w1_ff2, w3_ff2,
              w2_ff2, g_final):
    B, S, D = x.shape
    split_idx = [int(c) for c in np.cumsum(IN_SPLITS)]
    t5_a = t5_table[:, :A_HEADS]
    t5_c = t5_table[:, A_HEADS:]
    for l in range(DEPTH):
        x = x + 0.5 * swiglu(rmsnorm(x, g_ff1[l]), w1_ff1[l], w3_ff1[l], w2_ff1[l])
        h = rmsnorm(x, g_mix[l])
        p = h @ w_in[l]
        aq, ak, av, bq, bk, bv, cq, ck, cv, gates = jnp.split(p, split_idx, axis=-1)
        lam_init = 0.8 - 0.6 * math.exp(-0.3 * l)
        lam_full = (jnp.exp(jnp.sum(lam_q1[l] * lam_k1[l])) - jnp.exp(jnp.sum(lam_q2[l] * lam_k2[l]))
                    + lam_init)
        o_a = diff_attention(aq.reshape(B, S, A_HEADS, 2, HEAD_DIM),
                             ak.reshape(B, S, A_HEADS, 2, HEAD_DIM),
                             av.reshape(B, S, A_HEADS, 2 * HEAD_DIM),
                             t5_a, lam_full, subln_g[l], lam_init)
        hd = lambda t, nh: t.reshape(B, S, nh, HEAD_DIM)
        o_b = neighborhood_attention(hd(bq, B_HEADS), hd(bk, B_HEADS), hd(bv, B_HEADS), na_rpb[l])
        o_c = dilated_attention(hd(cq, C_HEADS), hd(ck, C_HEADS), hd(cv, C_HEADS), t5_c)
        g = jax.nn.sigmoid(gates.reshape(B, S, N_BRANCH, D))
        merged = (g[:, :, 0] * (o_a @ w_br_a[l]) + g[:, :, 1] * (o_b @ w_br_b[l])
                  + g[:, :, 2] * (o_c @ w_br_c[l]))
        x = x + merged @ w_o[l]
        x = x + 0.5 * swiglu(rmsnorm(x, g_ff2[l]), w1_ff2[l], w3_ff2[l], w2_ff2[l])
    return rmsnorm(x, g_final)
```

```python
import functools
import math

import numpy as np
import jax
import jax.numpy as jnp
from jax import lax
from jax.experimental import pallas as pl
from jax.experimental.pallas import tpu as pltpu

HEAD_DIM = 64
A_HEADS = 4
B_HEADS = 8
C_HEADS = 8
GRID_W = 64
NA_ROWS_MAX = 8
NA_COLS = 16
C_CONFIGS = ((128, 1), (512, 4), (2048, 16))
T5_BUCKETS = 32
T5_MAX_DIST = 1024
N_BRANCH = 3
EPS = 1e-6

LANES = 128
VMEM_LIMIT_BYTES = 56 * 1024 * 1024

MASKED = -1e30
BF16 = jnp.bfloat16
F32 = jnp.float32

PAIR_W = 2 * HEAD_DIM
QKV_COLS = 9 * 512
COLBLK = {"aq": 0, "ak": 4, "av": 8, "bq": 12, "bk": 16, "bv": 20, "cq": 24, "ck": 28, "cv": 32}
N_COLBLK = QKV_COLS // PAIR_W


def _params(*sem):
    return pltpu.CompilerParams(dimension_semantics=sem, vmem_limit_bytes=VMEM_LIMIT_BYTES)


def _rms(x, g):
    return x * lax.rsqrt(jnp.mean(x * x, axis=-1, keepdims=True) + EPS) * g


def _ffn_kernel(x_ref, g_ref, w1_ref, w3_ref, w2_ref, gf_ref, o_ref, h_ref, acc_ref, *, final_norm):
    j = pl.program_id(1)

    @pl.when(j == 0)
    def _():
        h_ref[...] = _rms(x_ref[...], g_ref[...]).astype(BF16)
        acc_ref[...] = jnp.zeros_like(acc_ref)

    h = h_ref[...]
    a = jnp.dot(h, w1_ref[...], preferred_element_type=F32)
    b = jnp.dot(h, w3_ref[...], preferred_element_type=F32)
    u = (a * jax.nn.sigmoid(a) * b).astype(BF16)
    acc_ref[...] += jnp.dot(u, w2_ref[...], preferred_element_type=F32)

    @pl.when(j == pl.num_programs(1) - 1)
    def _():
        y = x_ref[...] + 0.5 * acc_ref[...]
        if final_norm:
            y = _rms(y, gf_ref[...])
        o_ref[...] = y


def _ffn(x2, g, w1, w3, w2, g_final, *, final_norm, tm, tf):
    n, d = x2.shape
    ff = w1.shape[1]
    return pl.pallas_call(
        functools.partial(_ffn_kernel, final_norm=final_norm),
        grid=(n // tm, ff // tf),
        in_specs=[
            pl.BlockSpec((tm, d), lambda i, j: (i, 0)),
            pl.BlockSpec((1, d), lambda i, j: (0, 0)),
            pl.BlockSpec((d, tf), lambda i, j: (0, j)),
            pl.BlockSpec((d, tf), lambda i, j: (0, j)),
            pl.BlockSpec((tf, d), lambda i, j: (j, 0)),
            pl.BlockSpec((1, d), lambda i, j: (0, 0)),
        ],
        out_specs=pl.BlockSpec((tm, d), lambda i, j: (i, 0)),
        out_shape=jax.ShapeDtypeStruct((n, d), F32),
        scratch_shapes=[pltpu.VMEM((tm, d), BF16), pltpu.VMEM((tm, d), F32)],
        compiler_params=_params("parallel", "arbitrary"),
        name="ffn",
    )(x2, g.reshape(1, d), w1, w3, w2, g_final.reshape(1, d))


def _proj_kernel(x_ref, g_ref, w_ref, o_ref, *, tn):
    h = _rms(x_ref[...], g_ref[...]).astype(BF16)
    for c in range(o_ref.shape[1] // tn):
        o_ref[:, c * tn:(c + 1) * tn] = jnp.dot(
            h, w_ref[:, c * tn:(c + 1) * tn], preferred_element_type=F32).astype(BF16)


def _proj(x2, g, w_qkv, *, tm, tn):
    n, d = x2.shape
    cols = w_qkv.shape[1]
    return pl.pallas_call(
        functools.partial(_proj_kernel, tn=tn),
        grid=(n // tm,),
        in_specs=[
            pl.BlockSpec((tm, d), lambda i: (i, 0)),
            pl.BlockSpec((1, d), lambda i: (0, 0)),
            pl.BlockSpec((d, cols), lambda i: (0, 0)),
        ],
        out_specs=pl.BlockSpec((tm, cols), lambda i: (i, 0)),
        out_shape=jax.ShapeDtypeStruct((n, cols), BF16),
        compiler_params=_params("parallel"),
        name="qkv_proj",
    )(x2, g.reshape(1, d), w_qkv)


def _stack_masked_q(q):
    lane = lax.broadcasted_iota(jnp.int32, q.shape, 1)
    zero = jnp.zeros_like(q)
    return jnp.concatenate([jnp.where(lane < HEAD_DIM, q, zero),
                            jnp.where(lane >= HEAD_DIM, q, zero)], axis=0)


def _scores(qz, k):
    return lax.dot_general(qz, k, (((1,), (1,)), ((), ())), preferred_element_type=F32)


def _pick_head_lanes(lo, hi):
    lane = lax.broadcasted_iota(jnp.int32, lo.shape, 1)
    return jnp.where(lane < HEAD_DIM, lo, hi)


def _attn_a_kernel(far_ref, lam_ref, q_ref, k_ref, v_ref, bias_ref, g_ref, o_ref,
                   qz_ref, m_ref, l_ref, acc_ref, *, t, nn, nblk, lam_init):
    h = pl.program_id(1)
    i = pl.program_id(2)
    qz_ref[...] = _stack_masked_q(q_ref[...] * (HEAD_DIM ** -0.5))
    m_ref[...] = jnp.full_like(m_ref, -jnp.inf)
    l_ref[...] = jnp.zeros_like(l_ref)
    acc_ref[...] = jnp.zeros_like(acc_ref)

    def step(j, bias):
        start = pl.multiple_of(j * t, t)
        s = _scores(qz_ref[...], k_ref[pl.ds(start, t), :]) + bias
        m_prev = m_ref[...]
        m_new = jnp.maximum(m_prev, jnp.max(s, axis=-1, keepdims=True))
        alpha = jnp.exp(m_prev - m_new)
        p = jnp.exp(s - m_new)
        l_ref[...] = alpha * l_ref[...] + jnp.sum(p, axis=-1, keepdims=True)
        acc_ref[...] = alpha * acc_ref[...] + jnp.dot(
            p.astype(BF16), v_ref[pl.ds(start, t), :], preferred_element_type=F32)
        m_ref[...] = m_new

    def far_left(j, carry):
        step(j, far_ref[h, 0])
        return carry

    def far_right(j, carry):
        step(j, far_ref[h, 1])
        return carry

    lax.fori_loop(0, jnp.maximum(i - nn, 0), far_left, 0)
    for d in range(-nn, nn + 1):
        @pl.when(jnp.logical_and(i + d >= 0, i + d < nblk))
        def _(d=d):
            tile = bias_ref[d + nn]
            step(i + d, jnp.concatenate([tile, tile], axis=0))
    lax.fori_loop(jnp.minimum(i + nn + 1, nblk), nblk, far_right, 0)

    lam = (jnp.exp(jnp.sum(lam_ref[0:1, :] * lam_ref[1:2, :], axis=-1, keepdims=True))
           - jnp.exp(jnp.sum(lam_ref[2:3, :] * lam_ref[3:4, :], axis=-1, keepdims=True)) + lam_init)
    o = acc_ref[...] / l_ref[...]
    o = o[:t] - lam * o[t:]
    o_ref[...] = (_rms(o, g_ref[...]) * (1.0 - lam_init)).astype(BF16)


def _attn_a(qkv3, far, lam, bias_tiles, subln_g, *, t, nn, lam_init):
    b, s, _ = qkv3.shape
    nblk = s // t
    nd = 2 * nn + 1
    return pl.pallas_call(
        functools.partial(_attn_a_kernel, t=t, nn=nn, nblk=nblk, lam_init=lam_init),
        grid=(b, A_HEADS, nblk),
        in_specs=[
            pl.BlockSpec(memory_space=pltpu.SMEM),
            pl.BlockSpec((4, HEAD_DIM), lambda bi, h, i: (0, 0)),
            pl.BlockSpec((None, t, PAIR_W), lambda bi, h, i: (bi, i, COLBLK["aq"] + h)),
            pl.BlockSpec((None, s, PAIR_W), lambda bi, h, i: (bi, 0, COLBLK["ak"] + h)),
            pl.BlockSpec((None, s, PAIR_W), lambda bi, h, i: (bi, 0, COLBLK["av"] + h)),
            pl.BlockSpec((None, nd, t, t), lambda bi, h, i: (h, 0, 0, 0)),
            pl.BlockSpec((1, PAIR_W), lambda bi, h, i: (0, 0)),
        ],
        out_specs=pl.BlockSpec((None, t, PAIR_W), lambda bi, h, i: (bi, i, h)),
        out_shape=jax.ShapeDtypeStruct((b, s, A_HEADS * PAIR_W), BF16),
        scratch_shapes=[
            pltpu.VMEM((2 * t, PAIR_W), BF16),
            pltpu.VMEM((2 * t, 1), F32),
            pltpu.VMEM((2 * t, 1), F32),
            pltpu.VMEM((2 * t, PAIR_W), F32),
        ],
        compiler_params=_params("parallel", "parallel", "arbitrary"),
        name="attn_diff",
    )(far, lam, qkv3, qkv3, qkv3, bias_tiles, subln_g.reshape(1, PAIR_W))


def _attn_b_kernel(q_ref, k_ref, v_ref, bm_ref, o_ref, *, tq, tk, s_len):
    g = pl.program_id(2)
    start = pl.multiple_of(jnp.clip(g * tq - (NA_ROWS_MAX // 2) * GRID_W, 0, s_len - tk), GRID_W)
    qz = _stack_masked_q(q_ref[...] * (HEAD_DIM ** -0.5))
    sc = _scores(qz, k_ref[pl.ds(start, tk), :]) + bm_ref[...]
    m = jnp.max(sc, axis=-1, keepdims=True)
    p = jnp.exp(sc - m)
    l = jnp.sum(p, axis=-1, keepdims=True)
    o = jnp.dot(p.astype(BF16), v_ref[pl.ds(start, tk), :], preferred_element_type=F32) / l
    o_ref[...] = _pick_head_lanes(o[:tq], o[tq:]).astype(BF16)


def _attn_b(qkv3, bm, *, qrows):
    b, s, _ = qkv3.shape
    tq = qrows * GRID_W
    tk = (qrows + NA_ROWS_MAX) * GRID_W
    ngrp = s // tq
    npair = B_HEADS // 2

    def variant(g):
        return (g > 0).astype(jnp.int32) + (g == ngrp - 1).astype(jnp.int32)

    return pl.pallas_call(
        functools.partial(_attn_b_kernel, tq=tq, tk=tk, s_len=s),
        grid=(b, npair, ngrp),
        in_specs=[
            pl.BlockSpec((None, tq, PAIR_W), lambda bi, pr, g: (bi, g, COLBLK["bq"] + pr)),
            pl.BlockSpec((None, s, PAIR_W), lambda bi, pr, g: (bi, 0, COLBLK["bk"] + pr)),
            pl.BlockSpec((None, s, PAIR_W), lambda bi, pr, g: (bi, 0, COLBLK["bv"] + pr)),
            pl.BlockSpec((None, None, 2 * tq, tk), lambda bi, pr, g: (pr, variant(g), 0, 0)),
        ],
        out_specs=pl.BlockSpec((None, tq, PAIR_W), lambda bi, pr, g: (bi, g, pr)),
        out_shape=jax.ShapeDtypeStruct((b, s, B_HEADS * HEAD_DIM), BF16),
        compiler_params=_params("parallel", "parallel", "arbitrary"),
        name="attn_nbr",
    )(qkv3, qkv3, qkv3, bm)


def _attn_c_kernel(q_ref, k_ref, v_ref, bm_ref, o_ref, lse_ref, *, tq, half, s_r):
    tk = tq + 2 * half
    nblk = s_r // tq

    def body(i, carry):
        q0 = pl.multiple_of(i * tq, tq)
        ws = pl.multiple_of(jnp.clip(i * tq - half, 0, s_r - tk), half)
        var = (i > 0).astype(jnp.int32) + (i == nblk - 1).astype(jnp.int32)
        qz = _stack_masked_q(q_ref[pl.ds(q0, tq), :] * (HEAD_DIM ** -0.5))
        sc = _scores(qz, k_ref[pl.ds(ws, tk), :]) + bm_ref[var]
        m = jnp.max(sc, axis=-1, keepdims=True)
        p = jnp.exp(sc - m)
        l = jnp.sum(p, axis=-1, keepdims=True)
        o = jnp.dot(p.astype(BF16), v_ref[pl.ds(ws, tk), :], preferred_element_type=F32) / l
        lse = jnp.broadcast_to(m + jnp.log(l), o.shape)
        o_ref[pl.ds(q0, tq), :] = _pick_head_lanes(o[:tq], o[tq:]).astype(BF16)
        lse_ref[pl.ds(q0, tq), :] = _pick_head_lanes(lse[:tq], lse[tq:])
        return carry

    lax.fori_loop(0, nblk, body, 0)


def _attn_c(qkv3, bm, *, r, tq, half):
    b, s, cols = qkv3.shape
    s_r = s // r
    npair = C_HEADS // 2
    width = C_HEADS * HEAD_DIM
    qkv_r = qkv3.reshape(b, s_r, r * cols)
    ncb = cols // PAIR_W
    nob = width // PAIR_W

    def in_spec(name):
        return pl.BlockSpec((None, s_r, PAIR_W),
                            lambda bi, rho, pr: (bi, 0, rho * ncb + COLBLK[name] + pr))

    out_spec = pl.BlockSpec((None, s_r, PAIR_W), lambda bi, rho, pr: (bi, 0, rho * nob + pr))
    o, lse = pl.pallas_call(
        functools.partial(_attn_c_kernel, tq=tq, half=half, s_r=s_r),
        grid=(b, r, npair),
        in_specs=[
            in_spec("cq"), in_spec("ck"), in_spec("cv"),
            pl.BlockSpec((None, 3, 2 * tq, tq + 2 * half), lambda bi, rho, pr: (pr, 0, 0, 0)),
        ],
        out_specs=[out_spec, out_spec],
        out_shape=[jax.ShapeDtypeStruct((b, s_r, r * width), BF16),
                   jax.ShapeDtypeStruct((b, s_r, r * width), F32)],
        compiler_params=_params("parallel", "parallel", "arbitrary"),
        name=f"attn_dil{r}",
    )(qkv_r, qkv_r, qkv_r, bm)
    return o.reshape(b * s, width), lse.reshape(b * s, width)


def _merge_kernel(x_ref, g_ref, oa_ref, ob_ref, oc1_ref, oc2_ref, oc3_ref, ls1_ref, ls2_ref, ls3_ref,
                  wg_ref, wa_ref, wb_ref, wc_ref, wo_ref, o_ref):
    x = x_ref[...]
    d = x.shape[1]
    h = _rms(x, g_ref[...]).astype(BF16)

    ls1, ls2, ls3 = ls1_ref[...], ls2_ref[...], ls3_ref[...]
    mx = jnp.maximum(jnp.maximum(ls1, ls2), ls3)
    e1, e2, e3 = jnp.exp(ls1 - mx), jnp.exp(ls2 - mx), jnp.exp(ls3 - mx)
    oc = (e1 * oc1_ref[...].astype(F32) + e2 * oc2_ref[...].astype(F32)
          + e3 * oc3_ref[...].astype(F32)) / (e1 + e2 + e3)

    merged = jnp.zeros(x.shape, F32)
    for n, (br, w_ref) in enumerate(((oa_ref[...], wa_ref), (ob_ref[...], wb_ref),
                                     (oc.astype(BF16), wc_ref))):
        gate = jnp.dot(h, wg_ref[:, n * d:(n + 1) * d], preferred_element_type=F32)
        merged = merged + jax.nn.sigmoid(gate) * jnp.dot(br, w_ref[...], preferred_element_type=F32)
    o_ref[...] = x + jnp.dot(merged.astype(BF16), wo_ref[...], preferred_element_type=F32)


def _merge(x2, g, oa, ob, ocs, lses, wg, wa, wb, wc, wo, *, tm):
    n, d = x2.shape
    wbr = oa.shape[1]
    tok = lambda w: pl.BlockSpec((tm, w), lambda i: (i, 0))
    full = lambda a: pl.BlockSpec(a.shape, lambda i: (0, 0))
    return pl.pallas_call(
        _merge_kernel,
        grid=(n // tm,),
        in_specs=[tok(d), pl.BlockSpec((1, d), lambda i: (0, 0)), tok(wbr), tok(wbr),
                  tok(wbr), tok(wbr), tok(wbr), tok(wbr), tok(wbr), tok(wbr),
                  full(wg), full(wa), full(wb), full(wc), full(wo)],
        out_specs=tok(d),
        out_shape=jax.ShapeDtypeStruct((n, d), F32),
        compiler_params=_params("parallel"),
        name="gated_merge",
    )(x2, g.reshape(1, d), oa, ob, *ocs, *lses, wg, wa, wb, wc, wo)


def _t5_bucket_np(rel):
    nb = T5_BUCKETS // 2
    max_exact = nb // 2
    rel = np.asarray(rel, np.int64)
    n = np.abs(rel)
    x = np.log(np.maximum(n, 1) / max_exact) / math.log(T5_MAX_DIST / max_exact) * (nb - max_exact)
    interior = (n > max_exact) & (n < T5_MAX_DIST)
    assert np.all(np.abs(x[interior] - np.round(x[interior])) > 1e-6)
    large = np.minimum(max_exact + np.floor(x + 1e-9).astype(np.int64), nb - 1)
    return np.where(rel > 0, nb, 0) + np.where(n < max_exact, n, large)


def _a_near_blocks(t, s):
    rel = np.arange(1, s)
    buckets = _t5_bucket_np(rel)
    sat = buckets[-1]
    first_sat = int(rel[np.nonzero(buckets != sat)[0][-1]] + 1) if np.any(buckets != sat) else 1
    nn = 0
    while nn * t + 1 < first_sat:
        nn += 1
    return nn


def _a_bias_inputs(t5_a, t, s):
    nn = _a_near_blocks(t, s)
    qq = np.arange(t)[:, None]
    kk = np.arange(t)[None, :]
    idx = np.stack([_t5_bucket_np(d * t + kk - qq) for d in range(-nn, nn + 1)])
    tiles = jnp.transpose(t5_a[idx], (3, 0, 1, 2))
    sat = np.array([_t5_bucket_np(-(s - 1)), _t5_bucket_np(s - 1)])
    far = jnp.transpose(t5_a[sat], (1, 0))
    return tiles.astype(F32), far.astype(F32), nn


def _b_bias_tiles(rpb, qrows, rows):
    kh = min(NA_ROWS_MAX, rows)
    kw = min(NA_COLS, GRID_W)
    krows = qrows + NA_ROWS_MAX
    ngrp = rows // qrows
    tq, tk = qrows * GRID_W, krows * GRID_W
    q = np.arange(tq)
    k = np.arange(tk)
    qr, qc = q // GRID_W, q % GRID_W
    kr, kc = k // GRID_W, k % GRID_W
    cstart = np.clip(qc - kw // 2, 0, GRID_W - kw)
    col_ok = (kc[None, :] >= cstart[:, None]) & (kc[None, :] < cstart[:, None] + kw)
    rel_c = np.clip(kc[None, :] - qc[:, None] + (NA_COLS - 1), 0, 2 * NA_COLS - 2)
    valid, rel_r = [], []
    for g in (0, 1, ngrp - 1):
        r_abs = g * qrows + qr
        k0 = int(np.clip(g * qrows - NA_ROWS_MAX // 2, 0, rows - krows))
        kr_abs = k0 + kr
        rstart = np.clip(r_abs - kh // 2, 0, rows - kh)
        row_ok = (kr_abs[None, :] >= rstart[:, None]) & (kr_abs[None, :] < rstart[:, None] + kh)
        valid.append(row_ok & col_ok)
        rel_r.append(np.clip(kr_abs[None, :] - r_abs[:, None] + (NA_ROWS_MAX - 1), 0, 2 * NA_ROWS_MAX - 2))
    valid = np.stack(valid)
    flat = np.stack(rel_r) * (2 * NA_COLS - 1) + rel_c[None]
    vals = rpb.reshape(B_HEADS, -1)[:, flat]
    tiles = jnp.where(valid[None], vals, MASKED).astype(F32)
    tiles = tiles.reshape(B_HEADS // 2, 2, 3, tq, tk).transpose(0, 2, 1, 3, 4)
    return tiles.reshape(B_HEADS // 2, 3, 2 * tq, tk)


def _c_bias_tiles(t5_c, r, tq, half):
    tk = tq + 2 * half
    qq = np.arange(tq)[:, None]
    kk = np.arange(tk)[None, :]
    jj = np.stack([kk - qq + off for off in (0, -half, -2 * half)])
    valid = np.abs(jj) <= half
    idx = _t5_bucket_np(r * np.clip(jj, -half, half))
    vals = jnp.transpose(t5_c[idx], (3, 0, 1, 2))
    tiles = jnp.where(valid[None], vals, MASKED).astype(F32)
    tiles = tiles.reshape(C_HEADS // 2, 2, 3, tq, tk).transpose(0, 2, 1, 3, 4)
    return tiles.reshape(C_HEADS // 2, 3, 2 * tq, tk)


def _tiles(n_tok, s, d_ff):
    return dict(
        ffn_tm=min(1024, n_tok), ffn_tf=256 if d_ff % 256 == 0 else d_ff,
        proj_tm=min(512, n_tok), proj_tn=512,
        a_t=min(512, s),
        b_qrows=4,
        c_tq=128,
        merge_tm=min(256, n_tok),
    )


def kernel(x, g_ff1, w1_ff1, w3_ff1, w2_ff1, g_mix, w_in, lam_q1, lam_k1, lam_q2, lam_k2, subln_g,
           na_rpb, t5_table, w_br_a, w_br_b, w_br_c, w_o, g_ff2, w1_ff2, w3_ff2, w2_ff2, g_final):
    b, s, d = x.shape
    depth = w_in.shape[0]
    n_tok = b * s
    rows = s // GRID_W
    tl = _tiles(n_tok, s, w1_ff1.shape[2])
    bf = lambda w: w.astype(BF16)

    t5_a = t5_table[:, :A_HEADS]
    t5_c = t5_table[:, A_HEADS:]
    a_tiles, a_far, a_nn = _a_bias_inputs(t5_a, tl["a_t"], s)
    c_cfgs = [(w // (2 * r), r) for (w, r) in C_CONFIGS]
    c_tiles = [_c_bias_tiles(t5_c, r, tl["c_tq"], half) for (half, r) in c_cfgs]

    x2 = x.reshape(n_tok, d)
    for l in range(depth):
        x2 = _ffn(x2, g_ff1[l], bf(w1_ff1[l]), bf(w3_ff1[l]), bf(w2_ff1[l]), g_final,
                  final_norm=False, tm=tl["ffn_tm"], tf=tl["ffn_tf"])

        qkv = _proj(x2, g_mix[l], bf(w_in[l, :, :QKV_COLS]), tm=tl["proj_tm"], tn=tl["proj_tn"])
        qkv3 = qkv.reshape(b, s, QKV_COLS)

        lam_init = 0.8 - 0.6 * math.exp(-0.3 * l)
        lam_vecs = jnp.stack([lam_q1[l], lam_k1[l], lam_q2[l], lam_k2[l]])
        o_a = _attn_a(qkv3, a_far, lam_vecs, a_tiles, subln_g[l],
                      t=tl["a_t"], nn=a_nn, lam_init=lam_init)
        o_b = _attn_b(qkv3, _b_bias_tiles(na_rpb[l], tl["b_qrows"], rows), qrows=tl["b_qrows"])
        c_out = [_attn_c(qkv3, c_tiles[n], r=r, tq=tl["c_tq"], half=half)
                 for n, (half, r) in enumerate(c_cfgs)]

        x2 = _merge(x2, g_mix[l], o_a.reshape(n_tok, -1), o_b.reshape(n_tok, -1),
                    [o for o, _ in c_out], [ls for _, ls in c_out],
                    bf(w_in[l, :, QKV_COLS:]), bf(w_br_a[l]), bf(w_br_b[l]), bf(w_br_c[l]), bf(w_o[l]),
                    tm=tl["merge_tm"])

        x2 = _ffn(x2, g_ff2[l], bf(w1_ff2[l]), bf(w3_ff2[l]), bf(w2_ff2[l]), g_final,
                  final_norm=(l == depth - 1), tm=tl["ffn_tm"], tf=tl["ffn_tf"])
    return x2.reshape(b, s, d)
```

```python
import functools
import math

import numpy as np
import jax
import jax.numpy as jnp
from jax import lax
from jax.experimental import pallas as pl
from jax.experimental.pallas import tpu as pltpu

HEAD_DIM = 64
A_HEADS = 4
B_HEADS = 8
C_HEADS = 8
GRID_W = 64
NA_ROWS_MAX = 8
NA_COLS = 16
C_CONFIGS = ((128, 1), (512, 4), (2048, 16))
T5_BUCKETS = 32
T5_MAX_DIST = 1024
EPS = 1e-6

LANES = 128
VMEM_LIMIT_BYTES = 56 * 1024 * 1024

MASKED = -1e30
BF16 = jnp.bfloat16
F32 = jnp.float32

PAIR_W = 2 * HEAD_DIM
A_W = A_HEADS * PAIR_W
B_W = B_HEADS * HEAD_DIM
C_W = C_HEADS * HEAD_DIM
NAT_BLK = {"aq": 0, "ak": 4, "bq": 8, "bk": 12, "bv": 16}
NAT_COLS = 5 * 512
C_BLK = {"cq": 0, "ck": 4, "cv": 8}
C_COLS = 3 * C_W


def _params(*sem):
    return pltpu.CompilerParams(dimension_semantics=sem, vmem_limit_bytes=VMEM_LIMIT_BYTES)


def _rms(x, g):
    return x * lax.rsqrt(jnp.mean(x * x, axis=-1, keepdims=True) + EPS) * g


def _t5_abs_bucket_np(n):
    nb = T5_BUCKETS // 2
    max_exact = nb // 2
    n = np.asarray(n, np.int64)
    x = np.log(np.maximum(n, 1) / max_exact) / math.log(T5_MAX_DIST / max_exact) * (nb - max_exact)
    interior = (n > max_exact) & (n < T5_MAX_DIST)
    assert np.all(np.abs(x[interior] - np.round(x[interior])) > 1e-6)
    large = np.minimum(max_exact + np.floor(x + 1e-9).astype(np.int64), nb - 1)
    return np.where(n < max_exact, n, large)


def _t5_change_points(max_n):
    b = _t5_abs_bucket_np(np.arange(max_n + 1))
    return [(0, int(b[0]))] + [(n, int(b[n])) for n in range(1, max_n + 1) if b[n] != b[n - 1]]


def _t5_saturation(s):
    return _t5_change_points(s - 1)[-1][0]


def _t5_bias(rel, tbl_ref, head, max_n):
    nb = T5_BUCKETS // 2
    cps = _t5_change_points(max_n)
    n = jnp.abs(rel)
    neg = jnp.full(rel.shape, tbl_ref[cps[-1][1], head], F32)
    pos = jnp.full(rel.shape, tbl_ref[nb + cps[-1][1], head], F32)
    for (n0, bkt), (n1, _) in reversed(list(zip(cps[:-1], cps[1:]))):
        below = n < n1
        neg = jnp.where(below, tbl_ref[bkt, head], neg)
        pos = jnp.where(below, tbl_ref[nb + bkt, head], pos)
    return jnp.where(rel > 0, pos, neg)


def _ffn_kernel(x_ref, g_ref, w1_ref, w3_ref, w2_ref, gf_ref, o_ref, h_ref, acc_ref, *, final_norm):
    j = pl.program_id(1)

    @pl.when(j == 0)
    def _():
        h_ref[...] = _rms(x_ref[...], g_ref[...]).astype(BF16)
        acc_ref[...] = jnp.zeros_like(acc_ref)

    h = h_ref[...]
    a = jnp.dot(h, w1_ref[...], preferred_element_type=F32)
    b = jnp.dot(h, w3_ref[...], preferred_element_type=F32)
    u = (a * jax.nn.sigmoid(a) * b).astype(BF16)
    acc_ref[...] += jnp.dot(u, w2_ref[...], preferred_element_type=F32)

    @pl.when(j == pl.num_programs(1) - 1)
    def _():
        y = x_ref[...] + 0.5 * acc_ref[...]
        if final_norm:
            y = _rms(y, gf_ref[...])
        o_ref[...] = y


def _ffn(x2, g, w1, w3, w2, g_final, *, final_norm, tm, tf):
    n, d = x2.shape
    ff = w1.shape[1]
    return pl.pallas_call(
        functools.partial(_ffn_kernel, final_norm=final_norm),
        grid=(n // tm, ff // tf),
        in_specs=[
            pl.BlockSpec((tm, d), lambda i, j: (i, 0)),
            pl.BlockSpec((1, d), lambda i, j: (0, 0)),
            pl.BlockSpec((d, tf), lambda i, j: (0, j)),
            pl.BlockSpec((d, tf), lambda i, j: (0, j)),
            pl.BlockSpec((tf, d), lambda i, j: (j, 0)),
            pl.BlockSpec((1, d), lambda i, j: (0, 0)),
        ],
        out_specs=pl.BlockSpec((tm, d), lambda i, j: (i, 0)),
        out_shape=jax.ShapeDtypeStruct((n, d), F32),
        scratch_shapes=[pltpu.VMEM((tm, d), BF16), pltpu.VMEM((tm, d), F32)],
        compiler_params=_params("parallel", "arbitrary"),
        name="ffn",
    )(x2, g.reshape(1, d), w1, w3, w2, g_final.reshape(1, d))


def _proj_kernel(x_ref, g_ref, wn_ref, wvt_ref, wc_ref, nat_ref, avt_ref, *rest, tn, dils):
    c_refs, cs_ref = rest[:-1], rest[-1]
    tm = x_ref.shape[0]
    h = _rms(x_ref[...], g_ref[...]).astype(BF16)
    for c in range(NAT_COLS // tn):
        nat_ref[:, c * tn:(c + 1) * tn] = jnp.dot(
            h, wn_ref[:, c * tn:(c + 1) * tn], preferred_element_type=F32).astype(BF16)
    avt_ref[...] = lax.dot_general(wvt_ref[...], h, (((1,), (1,)), ((), ())),
                                   preferred_element_type=F32).astype(BF16)
    per = tn // LANES
    for c in range(C_COLS // tn):
        res = jnp.dot(h, wc_ref[:, c * tn:(c + 1) * tn], preferred_element_type=F32)
        for k in range(per):
            cs_ref[c * per + k] = res[:, k * LANES:(k + 1) * LANES]
    for c_ref, r in zip(c_refs, dils):
        for rho in range(r):
            for cb in range(C_COLS // LANES):
                rows = cs_ref[cb] if r == 1 else cs_ref[cb, pl.ds(rho, tm // r, stride=r), :]
                c_ref[:, rho * C_COLS + cb * LANES:rho * C_COLS + (cb + 1) * LANES] = rows.astype(BF16)


def _proj(x3, g, w_nat, w_avt, w_c, *, tm, tn, dils):
    b, s, d = x3.shape
    nt = s // tm
    tok = lambda w: pl.BlockSpec((None, tm, w), lambda bi, i: (bi, i, 0))
    full = lambda a: pl.BlockSpec(a.shape, lambda bi, i: (0, 0))
    out_shapes = [jax.ShapeDtypeStruct((b, s, NAT_COLS), BF16),
                  jax.ShapeDtypeStruct((b, nt, A_W, tm), BF16)]
    out_specs = [tok(NAT_COLS), pl.BlockSpec((None, None, A_W, tm), lambda bi, i: (bi, i, 0, 0))]
    for r in dils:
        out_shapes.append(jax.ShapeDtypeStruct((b, s // r, r * C_COLS), BF16))
        out_specs.append(pl.BlockSpec((None, tm // r, r * C_COLS), lambda bi, i: (bi, i, 0)))
    return pl.pallas_call(
        functools.partial(_proj_kernel, tn=tn, dils=dils),
        grid=(b, nt),
        in_specs=[tok(d), pl.BlockSpec((1, d), lambda bi, i: (0, 0)), full(w_nat), full(w_avt), full(w_c)],
        out_specs=out_specs,
        out_shape=out_shapes,
        scratch_shapes=[pltpu.VMEM((C_COLS // LANES, tm, LANES), F32)],
        compiler_params=_params("parallel", "parallel"),
        name="qkv_proj",
    )(x3, g.reshape(1, d), w_nat, w_avt, w_c)


def _stack_masked_q(q):
    lane = lax.broadcasted_iota(jnp.int32, q.shape, 1)
    zero = jnp.zeros_like(q)
    return jnp.concatenate([jnp.where(lane < HEAD_DIM, q, zero),
                            jnp.where(lane >= HEAD_DIM, q, zero)], axis=0)


def _nt_dot(a, b):
    return lax.dot_general(a, b, (((1,), (1,)), ((), ())), preferred_element_type=F32)


def _pick_head_lanes(lo, hi):
    lane = lax.broadcasted_iota(jnp.int32, lo.shape, 1)
    return jnp.where(lane < HEAD_DIM, lo, hi)


def _attn_a_kernel(tbl_ref, lam_ref, q_ref, k_ref, vt_ref, g_ref, o_ref,
                   qz_ref, m_ref, l_ref, acc_ref, bias_ref, *, t, nn, nblk, lam_init):
    h = pl.program_id(0)
    bi = pl.program_id(1)
    i = pl.program_id(2)
    nb = T5_BUCKETS // 2
    sat = _t5_change_points((nn + 1) * t)[-1][1]

    @pl.when(jnp.logical_and(bi == 0, i == 0))
    def _():
        kk = lax.broadcasted_iota(jnp.int32, (t, t), 0)
        qq = lax.broadcasted_iota(jnp.int32, (t, t), 1)
        for d in range(-nn, nn + 1):
            bias_ref[d + nn] = _t5_bias(d * t + kk - qq, tbl_ref, h, (nn + 1) * t)

    qz_ref[...] = _stack_masked_q(q_ref[...] * (HEAD_DIM ** -0.5))
    m_ref[...] = jnp.full_like(m_ref, -jnp.inf)
    l_ref[...] = jnp.zeros_like(l_ref)
    acc_ref[...] = jnp.zeros_like(acc_ref)

    def step(j, shift, tile):
        start = pl.multiple_of(j * t, t)
        kb = k_ref[pl.ds(start, t), :]
        vtb = vt_ref[j]
        for half in range(2):
            cols = slice(half * t, (half + 1) * t)
            s = _nt_dot(kb, qz_ref[cols, :])
            if tile is not None:
                s = s + tile
            m_prev = m_ref[:, cols]
            m_new = jnp.maximum(m_prev, jnp.max(s, axis=0, keepdims=True) + shift)
            alpha = jnp.exp(m_prev - m_new)
            p = jnp.exp(s - (m_new - shift))
            l_ref[:, cols] = alpha * l_ref[:, cols] + jnp.sum(p, axis=0, keepdims=True)
            acc_ref[:, cols] = alpha * acc_ref[:, cols] + jnp.dot(
                vtb, p.astype(BF16), preferred_element_type=F32)
            m_ref[:, cols] = m_new

    def far_left(j, carry):
        step(j, tbl_ref[sat, h], None)
        return carry

    def far_right(j, carry):
        step(j, tbl_ref[nb + sat, h], None)
        return carry

    lax.fori_loop(0, jnp.maximum(i - nn, 0), far_left, 0)
    for d in range(-nn, nn + 1):
        @pl.when(jnp.logical_and(i + d >= 0, i + d < nblk))
        def _(d=d):
            step(i + d, 0.0, bias_ref[d + nn])
    lax.fori_loop(jnp.minimum(i + nn + 1, nblk), nblk, far_right, 0)

    lam = (jnp.exp(jnp.sum(lam_ref[0:1, :] * lam_ref[1:2, :], axis=-1, keepdims=True))
           - jnp.exp(jnp.sum(lam_ref[2:3, :] * lam_ref[3:4, :], axis=-1, keepdims=True)) + lam_init)
    o = acc_ref[...] / l_ref[...]
    o = o[:, :t] - lam * o[:, t:]
    o = o * lax.rsqrt(jnp.mean(o * o, axis=0, keepdims=True) + EPS) * g_ref[...] * (1.0 - lam_init)
    o_ref[...] = o.T.astype(BF16)


def _attn_a(nat, avt, t5_table, lam, subln_g, *, t, lam_init):
    b, s, _ = nat.shape
    nblk = s // t
    sat_n = _t5_saturation(s)
    nn = -(-(sat_n - 1) // t)
    nd = 2 * nn + 1
    return pl.pallas_call(
        functools.partial(_attn_a_kernel, t=t, nn=nn, nblk=nblk, lam_init=lam_init),
        grid=(A_HEADS, b, nblk),
        in_specs=[
            pl.BlockSpec(memory_space=pltpu.SMEM),
            pl.BlockSpec((4, HEAD_DIM), lambda h, bi, i: (0, 0)),
            pl.BlockSpec((None, t, PAIR_W), lambda h, bi, i: (bi, i, NAT_BLK["aq"] + h)),
            pl.BlockSpec((None, s, PAIR_W), lambda h, bi, i: (bi, 0, NAT_BLK["ak"] + h)),
            pl.BlockSpec((None, nblk, PAIR_W, t), lambda h, bi, i: (bi, 0, h, 0)),
            pl.BlockSpec((PAIR_W, 1), lambda h, bi, i: (0, 0)),
        ],
        out_specs=pl.BlockSpec((None, t, PAIR_W), lambda h, bi, i: (bi, i, h)),
        out_shape=jax.ShapeDtypeStruct((b, s, A_W), BF16),
        scratch_shapes=[
            pltpu.VMEM((2 * t, PAIR_W), BF16),
            pltpu.VMEM((1, 2 * t), F32),
            pltpu.VMEM((1, 2 * t), F32),
            pltpu.VMEM((PAIR_W, 2 * t), F32),
            pltpu.VMEM((nd, t, t), F32),
        ],
        compiler_params=_params("arbitrary", "arbitrary", "arbitrary"),
        name="attn_diff",
    )(t5_table, lam, nat, nat, avt, subln_g.reshape(PAIR_W, 1))


def _attn_b_kernel(q_ref, k_ref, v_ref, bm_ref, o_ref, *, tq, tk, s_len):
    g = pl.program_id(2)
    start = pl.multiple_of(jnp.clip(g * tq - (NA_ROWS_MAX // 2) * GRID_W, 0, s_len - tk), GRID_W)
    qz = _stack_masked_q(q_ref[...] * (HEAD_DIM ** -0.5))
    sc = _nt_dot(qz, k_ref[pl.ds(start, tk), :]) + bm_ref[...]
    m = jnp.max(sc, axis=-1, keepdims=True)
    p = jnp.exp(sc - m)
    l = jnp.sum(p, axis=-1, keepdims=True)
    o = jnp.dot(p.astype(BF16), v_ref[pl.ds(start, tk), :], preferred_element_type=F32) / l
    o_ref[...] = _pick_head_lanes(o[:tq], o[tq:]).astype(BF16)


def _attn_b(nat, bm, *, qrows):
    b, s, _ = nat.shape
    tq = qrows * GRID_W
    tk = (qrows + NA_ROWS_MAX) * GRID_W
    ngrp = s // tq
    npair = B_HEADS // 2

    def variant(g):
        return (g > 0).astype(jnp.int32) + (g == ngrp - 1).astype(jnp.int32)

    return pl.pallas_call(
        functools.partial(_attn_b_kernel, tq=tq, tk=tk, s_len=s),
        grid=(b, npair, ngrp),
        in_specs=[
            pl.BlockSpec((None, tq, PAIR_W), lambda bi, pr, g: (bi, g, NAT_BLK["bq"] + pr)),
            pl.BlockSpec((None, s, PAIR_W), lambda bi, pr, g: (bi, 0, NAT_BLK["bk"] + pr)),
            pl.BlockSpec((None, s, PAIR_W), lambda bi, pr, g: (bi, 0, NAT_BLK["bv"] + pr)),
            pl.BlockSpec((None, None, 2 * tq, tk), lambda bi, pr, g: (pr, variant(g), 0, 0)),
        ],
        out_specs=pl.BlockSpec((None, tq, PAIR_W), lambda bi, pr, g: (bi, g, pr)),
        out_shape=jax.ShapeDtypeStruct((b, s, B_W), BF16),
        compiler_params=_params("parallel", "parallel", "arbitrary"),
        name="attn_nbr",
    )(nat, nat, nat, bm)


def _b_bias_tiles(rpb, qrows, rows):
    kh = min(NA_ROWS_MAX, rows)
    kw = min(NA_COLS, GRID_W)
    krows = qrows + NA_ROWS_MAX
    ngrp = rows // qrows
    n_rel_r, n_rel_c = 2 * NA_ROWS_MAX - 1, 2 * NA_COLS - 1
    qc = np.arange(GRID_W)[:, None]
    kc = np.arange(GRID_W)[None, :]
    cstart = np.clip(qc - kw // 2, 0, GRID_W - kw)
    col_ok = (kc >= cstart) & (kc < cstart + kw)
    rel_c = kc - qc + (NA_COLS - 1)
    onehot_c = (rel_c[None] == np.arange(n_rel_c)[:, None, None]).astype(np.float32)
    blocks = jnp.einsum("hac,cqk->haqk", rpb.astype(F32), onehot_c, precision=lax.Precision.HIGHEST)
    blocks = jnp.where(col_ok[None, None], blocks, MASKED)
    blocks = jnp.concatenate([blocks, jnp.full((B_HEADS, 1, GRID_W, GRID_W), MASKED, F32)], axis=1)
    block_idx = []
    for g in (0, 1, ngrp - 1):
        r_abs = g * qrows + np.arange(qrows)[:, None]
        k0 = int(np.clip(g * qrows - NA_ROWS_MAX // 2, 0, rows - krows))
        kr_abs = k0 + np.arange(krows)[None, :]
        rstart = np.clip(r_abs - kh // 2, 0, rows - kh)
        row_ok = (kr_abs >= rstart) & (kr_abs < rstart + kh)
        block_idx.append(np.where(row_ok, kr_abs - r_abs + (NA_ROWS_MAX - 1), n_rel_r))
    block_idx = np.stack(block_idx)
    tiles = jnp.take(blocks, block_idx.reshape(-1), axis=1)
    tiles = tiles.reshape(B_HEADS // 2, 2, 3, qrows, krows, GRID_W, GRID_W)
    tiles = tiles.transpose(0, 2, 1, 3, 5, 4, 6)
    return tiles.reshape(B_HEADS // 2, 3, 2 * qrows * GRID_W, krows * GRID_W)


def _attn_c_kernel(tbl_ref, q_ref, k_ref, v_ref, o_ref, lse_ref, bm_ref, *, r, tq, half, s_r):
    tk = tq + 2 * half
    nblk = s_r // tq
    pr = pl.program_id(0)

    @pl.when(jnp.logical_and(pl.program_id(1) == 0, pl.program_id(2) == 0))
    def _():
        qq = lax.broadcasted_iota(jnp.int32, (tq, tk), 0)
        kk = lax.broadcasted_iota(jnp.int32, (tq, tk), 1)
        for var, off in enumerate((0, -half, -2 * half)):
            jj = kk - qq + off
            valid = jnp.abs(jj) <= half
            rel = r * jnp.clip(jj, -half, half)
            for hh in range(2):
                bias = _t5_bias(rel, tbl_ref, A_HEADS + 2 * pr + hh, r * half)
                bm_ref[var, hh * tq:(hh + 1) * tq, :] = jnp.where(valid, bias, MASKED)

    def body(i, carry):
        q0 = pl.multiple_of(i * tq, tq)
        ws = pl.multiple_of(jnp.clip(i * tq - half, 0, s_r - tk), half)
        var = (i > 0).astype(jnp.int32) + (i == nblk - 1).astype(jnp.int32)
        qz = _stack_masked_q(q_ref[pl.ds(q0, tq), :] * (HEAD_DIM ** -0.5))
        sc = _nt_dot(qz, k_ref[pl.ds(ws, tk), :]) + bm_ref[var]
        m = jnp.max(sc, axis=-1, keepdims=True)
        p = jnp.exp(sc - m)
        l = jnp.sum(p, axis=-1, keepdims=True)
        o = jnp.dot(p.astype(BF16), v_ref[pl.ds(ws, tk), :], preferred_element_type=F32) / l
        lse = jnp.broadcast_to(m + jnp.log(l), o.shape)
        o_ref[pl.ds(q0, tq), :] = _pick_head_lanes(o[:tq], o[tq:]).astype(BF16)
        lse_ref[pl.ds(q0, tq), :] = _pick_head_lanes(lse[:tq], lse[tq:])
        return carry

    lax.fori_loop(0, nblk, body, 0)


def _attn_c(c_r, t5_table, *, r, tq, half):
    b, s_r, _ = c_r.shape
    npair = C_HEADS // 2
    ncb = C_COLS // PAIR_W
    nob = C_W // PAIR_W

    def in_spec(name):
        return pl.BlockSpec((None, s_r, PAIR_W),
                            lambda pr, bi, rho: (bi, 0, rho * ncb + C_BLK[name] + pr))

    out_spec = pl.BlockSpec((None, s_r, PAIR_W), lambda pr, bi, rho: (bi, 0, rho * nob + pr))
    o, lse = pl.pallas_call(
        functools.partial(_attn_c_kernel, r=r, tq=tq, half=half, s_r=s_r),
        grid=(npair, b, r),
        in_specs=[pl.BlockSpec(memory_space=pltpu.SMEM), in_spec("cq"), in_spec("ck"), in_spec("cv")],
        out_specs=[out_spec, out_spec],
        out_shape=[jax.ShapeDtypeStruct((b, s_r, r * C_W), BF16),
                   jax.ShapeDtypeStruct((b, s_r, r * C_W), F32)],
        scratch_shapes=[pltpu.VMEM((3, 2 * tq, tq + 2 * half), F32)],
        compiler_params=_params("arbitrary", "arbitrary", "arbitrary"),
        name=f"attn_dil{r}",
    )(t5_table, c_r, c_r, c_r)
    return o.reshape(b * s_r, r * C_W), lse.reshape(b * s_r, r * C_W)


def _merge_kernel(x_ref, g_ref, oa_ref, ob_ref, *rest, dils):
    nc = len(dils)
    oc_refs, ls_refs = rest[:nc], rest[nc:2 * nc]
    wg_ref, wa_ref, wb_ref, wc_ref, wo_ref, o_ref, tok_ref = rest[2 * nc:]
    x = x_ref[...]
    tm, d = x.shape
    h = _rms(x, g_ref[...]).astype(BF16)

    def token_order(ref, r):
        if r == 1:
            return ref[...].astype(F32)
        for rho in range(r):
            for cb in range(C_W // LANES):
                lo = rho * C_W + cb * LANES
                tok_ref[cb, pl.ds(rho, tm // r, stride=r), :] = ref[:, lo:lo + LANES].astype(F32)
        return jnp.concatenate([tok_ref[cb] for cb in range(C_W // LANES)], axis=1)

    lses = [token_order(ref, r) for ref, r in zip(ls_refs, dils)]
    mx = functools.reduce(jnp.maximum, lses)
    es = [jnp.exp(ls - mx) for ls in lses]
    num = sum(e * token_order(ref, r) for e, ref, r in zip(es, oc_refs, dils))
    oc = num / sum(es)

    merged = jnp.zeros(x.shape, F32)
    for n, (br, w_ref) in enumerate(((oa_ref[...], wa_ref), (ob_ref[...], wb_ref),
                                     (oc.astype(BF16), wc_ref))):
        gate = jnp.dot(h, wg_ref[:, n * d:(n + 1) * d], preferred_element_type=F32)
        merged = merged + jax.nn.sigmoid(gate) * jnp.dot(br, w_ref[...], preferred_element_type=F32)
    o_ref[...] = x + jnp.dot(merged.astype(BF16), wo_ref[...], preferred_element_type=F32)


def _merge(x2, g, oa, ob, ocs, lses, wg, wa, wb, wc, wo, *, tm, dils):
    n, d = x2.shape
    tok = lambda w: pl.BlockSpec((tm, w), lambda i: (i, 0))
    dil = lambda r: pl.BlockSpec((tm // r, r * C_W), lambda i: (i, 0))
    full = lambda a: pl.BlockSpec(a.shape, lambda i: (0, 0))
    return pl.pallas_call(
        functools.partial(_merge_kernel, dils=dils),
        grid=(n // tm,),
        in_specs=[tok(d), pl.BlockSpec((1, d), lambda i: (0, 0)), tok(A_W), tok(B_W)]
                 + [dil(r) for r in dils] + [dil(r) for r in dils]
                 + [full(wg), full(wa), full(wb), full(wc), full(wo)],
        out_specs=tok(d),
        out_shape=jax.ShapeDtypeStruct((n, d), F32),
        scratch_shapes=[pltpu.VMEM((C_W // LANES, tm, LANES), F32)],
        compiler_params=_params("parallel"),
        name="gated_merge",
    )(x2, g.reshape(1, d), oa, ob, *ocs, *lses, wg, wa, wb, wc, wo)


def _tiles(n_tok, s, d_ff):
    return dict(
        ffn_tm=min(1024, n_tok), ffn_tf=256 if d_ff % 256 == 0 else d_ff,
        a_t=min(512, s),
        proj_tn=512,
        b_qrows=4,
        c_tq=128,
        merge_tm=min(256, n_tok),
    )


def kernel(x, g_ff1, w1_ff1, w3_ff1, w2_ff1, g_mix, w_in, lam_q1, lam_k1, lam_q2, lam_k2, subln_g,
           na_rpb, t5_table, w_br_a, w_br_b, w_br_c, w_o, g_ff2, w1_ff2, w3_ff2, w2_ff2, g_final):
    b, s, d = x.shape
    depth = w_in.shape[0]
    n_tok = b * s
    rows = s // GRID_W
    tl = _tiles(n_tok, s, w1_ff1.shape[2])
    bf = lambda w: w.astype(BF16)
    c_cfgs = [(w // (2 * r), r) for (w, r) in C_CONFIGS]
    dils = tuple(r for _, r in c_cfgs)
    t5_table = t5_table.astype(F32)

    col = lambda i0, i1: slice(512 * i0, 512 * i1)

    x2 = x.reshape(n_tok, d)
    for l in range(depth):
        x2 = _ffn(x2, g_ff1[l], bf(w1_ff1[l]), bf(w3_ff1[l]), bf(w2_ff1[l]), g_final,
                  final_norm=False, tm=tl["ffn_tm"], tf=tl["ffn_tf"])

        w = w_in[l]
        w_nat = bf(jnp.concatenate([w[:, col(0, 2)], w[:, col(3, 6)]], axis=1))
        w_avt = bf(w[:, col(2, 3)].T)
        w_c = bf(w[:, col(6, 9)])
        nat, avt, *c_rs = _proj(x2.reshape(b, s, d), g_mix[l], w_nat, w_avt, w_c,
                                tm=tl["a_t"], tn=tl["proj_tn"], dils=dils)

        lam_init = 0.8 - 0.6 * math.exp(-0.3 * l)
        lam_vecs = jnp.stack([lam_q1[l], lam_k1[l], lam_q2[l], lam_k2[l]])
        o_a = _attn_a(nat, avt, t5_table, lam_vecs, subln_g[l], t=tl["a_t"], lam_init=lam_init)
        o_b = _attn_b(nat, _b_bias_tiles(na_rpb[l], tl["b_qrows"], rows), qrows=tl["b_qrows"])
        c_out = [_attn_c(c_r, t5_table, r=r, tq=tl["c_tq"], half=half)
                 for c_r, (half, r) in zip(c_rs, c_cfgs)]

        x2 = _merge(x2, g_mix[l], o_a.reshape(n_tok, A_W), o_b.reshape(n_tok, B_W),
                    [o for o, _ in c_out], [ls for _, ls in c_out],
                    bf(w[:, col(9, 15)]), bf(w_br_a[l]), bf(w_br_b[l]), bf(w_br_c[l]), bf(w_o[l]),
                    tm=tl["merge_tm"], dils=dils)

        x2 = _ffn(x2, g_ff2[l], bf(w1_ff2[l]), bf(w3_ff2[l]), bf(w2_ff2[l]), g_final,
                  final_norm=(l == depth - 1), tm=tl["ffn_tm"], tf=tl["ffn_tf"])
    return x2.reshape(b, s, d)
```

```python
import functools
import math

import numpy as np
import jax
import jax.numpy as jnp
from jax import lax
from jax.experimental import pallas as pl
from jax.experimental.pallas import tpu as pltpu

HEAD_DIM = 64
A_HEADS = 4
B_HEADS = 8
C_HEADS = 8
GRID_W = 64
NA_ROWS_MAX = 8
NA_COLS = 16
C_CONFIGS = ((128, 1), (512, 4), (2048, 16))
T5_BUCKETS = 32
T5_MAX_DIST = 1024
EPS = 1e-6

LANES = 128
MXU_TILE = 256
VMEM_LIMIT_BYTES = 56 * 1024 * 1024

MASKED = -1e30
BF16 = jnp.bfloat16
F32 = jnp.float32

PAIR_W = 2 * HEAD_DIM
A_W = A_HEADS * PAIR_W
B_W = B_HEADS * HEAD_DIM
C_W = C_HEADS * HEAD_DIM
GROUP_W = 512
LOG2E = math.log2(math.e)
QK_SCALE = HEAD_DIM ** -0.5
NAT_BLK = {"aq": 0, "ak": 4, "bq": 8, "bk": 12, "bv": 16}
NAT_SCALE = (QK_SCALE * LOG2E, 1.0, QK_SCALE, 1.0, 1.0)
NAT_COLS = len(NAT_SCALE) * GROUP_W
C_BLK = {"cq": 0, "ck": 4, "cv": 8}
C_SCALE = (QK_SCALE, 1.0, 1.0)
C_COLS = len(C_SCALE) * GROUP_W
AVT_ROWS = PAIR_W + 16


def _params(*sem):
    return pltpu.CompilerParams(dimension_semantics=sem, vmem_limit_bytes=VMEM_LIMIT_BYTES)


def _rms(x, g):
    return x * lax.rsqrt(jnp.mean(x * x, axis=-1, keepdims=True) + EPS) * g


def _t5_abs_bucket_np(n):
    nb = T5_BUCKETS // 2
    max_exact = nb // 2
    n = np.asarray(n, np.int64)
    x = np.log(np.maximum(n, 1) / max_exact) / math.log(T5_MAX_DIST / max_exact) * (nb - max_exact)
    interior = (n > max_exact) & (n < T5_MAX_DIST)
    assert np.all(np.abs(x[interior] - np.round(x[interior])) > 1e-6)
    large = np.minimum(max_exact + np.floor(x + 1e-9).astype(np.int64), nb - 1)
    return np.where(n < max_exact, n, large)


def _t5_change_points(max_n):
    b = _t5_abs_bucket_np(np.arange(max_n + 1))
    return [(0, int(b[0]))] + [(n, int(b[n])) for n in range(1, max_n + 1) if b[n] != b[n - 1]]


def _t5_saturation(s):
    return _t5_change_points(s - 1)[-1][0]


def _t5_bias(rel, tbl_ref, head, max_n):
    nb = T5_BUCKETS // 2
    cps = _t5_change_points(max_n)
    n = jnp.abs(rel)
    neg = jnp.full(rel.shape, tbl_ref[cps[-1][1], head], F32)
    pos = jnp.full(rel.shape, tbl_ref[nb + cps[-1][1], head], F32)
    for (n0, bkt), (n1, _) in reversed(list(zip(cps[:-1], cps[1:]))):
        below = n < n1
        neg = jnp.where(below, tbl_ref[bkt, head], neg)
        pos = jnp.where(below, tbl_ref[nb + bkt, head], pos)
    return jnp.where(rel > 0, pos, neg)


def _ffn_kernel(x_ref, g_ref, w1_ref, w3_ref, w2_ref, gf_ref, o_ref, h_ref, acc_ref, *, final_norm):
    j = pl.program_id(1)

    @pl.when(j == 0)
    def _():
        h_ref[...] = _rms(x_ref[...], g_ref[...]).astype(BF16)
        acc_ref[...] = jnp.zeros_like(acc_ref)

    h = h_ref[...]
    a = jnp.dot(h, w1_ref[...], preferred_element_type=F32)
    b = jnp.dot(h, w3_ref[...], preferred_element_type=F32)
    u = (a * jax.nn.sigmoid(a) * b).astype(BF16)
    acc_ref[...] += jnp.dot(u, w2_ref[...], preferred_element_type=F32)

    @pl.when(j == pl.num_programs(1) - 1)
    def _():
        y = x_ref[...] + 0.5 * acc_ref[...]
        if final_norm:
            y = _rms(y, gf_ref[...])
        o_ref[...] = y


def _ffn(x2, g, w1, w3, w2, g_final, *, final_norm, tm, tf):
    n, d = x2.shape
    ff = w1.shape[1]
    return pl.pallas_call(
        functools.partial(_ffn_kernel, final_norm=final_norm),
        grid=(n // tm, ff // tf),
        in_specs=[
            pl.BlockSpec((tm, d), lambda i, j: (i, 0)),
            pl.BlockSpec((1, d), lambda i, j: (0, 0)),
            pl.BlockSpec((d, tf), lambda i, j: (0, j)),
            pl.BlockSpec((d, tf), lambda i, j: (0, j)),
            pl.BlockSpec((tf, d), lambda i, j: (j, 0)),
            pl.BlockSpec((1, d), lambda i, j: (0, 0)),
        ],
        out_specs=pl.BlockSpec((tm, d), lambda i, j: (i, 0)),
        out_shape=jax.ShapeDtypeStruct((n, d), F32),
        scratch_shapes=[pltpu.VMEM((tm, d), BF16), pltpu.VMEM((tm, d), F32)],
        compiler_params=_params("parallel", "arbitrary"),
        name="ffn",
    )(x2, g.reshape(1, d), w1, w3, w2, g_final.reshape(1, d))


def _proj_kernel(x_ref, g_ref, wn_ref, wvt_ref, wc_ref, nat_ref, avt_ref, *rest, dils):
    c_refs, cs_ref = rest[:-1], rest[-1]
    tm = x_ref.shape[0]
    tn = GROUP_W
    h = _rms(x_ref[...], g_ref[...]).astype(BF16)
    for c, scale in enumerate(NAT_SCALE):
        res = jnp.dot(h, wn_ref[:, c * tn:(c + 1) * tn], preferred_element_type=F32)
        nat_ref[:, c * tn:(c + 1) * tn] = (res if scale == 1.0 else res * scale).astype(BF16)
    vt = lax.dot_general(wvt_ref[...], h, (((1,), (1,)), ((), ())),
                         preferred_element_type=F32).astype(BF16)
    for hd in range(A_HEADS):
        avt_ref[hd * AVT_ROWS:hd * AVT_ROWS + PAIR_W, :] = vt[hd * PAIR_W:(hd + 1) * PAIR_W]
        avt_ref[hd * AVT_ROWS + PAIR_W:(hd + 1) * AVT_ROWS, :] = jnp.ones((AVT_ROWS - PAIR_W, tm), BF16)
    per = tn // LANES
    for c, scale in enumerate(C_SCALE):
        res = jnp.dot(h, wc_ref[:, c * tn:(c + 1) * tn], preferred_element_type=F32)
        res = res if scale == 1.0 else res * scale
        for k in range(per):
            cs_ref[c * per + k] = res[:, k * LANES:(k + 1) * LANES]
    for c_ref, r in zip(c_refs, dils):
        for rho in range(r):
            for cb in range(C_COLS // LANES):
                rows = cs_ref[cb] if r == 1 else cs_ref[cb, pl.ds(rho, tm // r, stride=r), :]
                c_ref[:, rho * C_COLS + cb * LANES:rho * C_COLS + (cb + 1) * LANES] = rows.astype(BF16)


def _proj(x3, g, w_nat, w_avt, w_c, *, tm, dils):
    b, s, d = x3.shape
    nt = s // tm
    tok = lambda w: pl.BlockSpec((None, tm, w), lambda bi, i: (bi, i, 0))
    full = lambda a: pl.BlockSpec(a.shape, lambda bi, i: (0, 0))
    out_shapes = [jax.ShapeDtypeStruct((b, s, NAT_COLS), BF16),
                  jax.ShapeDtypeStruct((b, nt, A_HEADS * AVT_ROWS, tm), BF16)]
    out_specs = [tok(NAT_COLS),
                 pl.BlockSpec((None, None, A_HEADS * AVT_ROWS, tm), lambda bi, i: (bi, i, 0, 0))]
    for r in dils:
        out_shapes.append(jax.ShapeDtypeStruct((b, s // r, r * C_COLS), BF16))
        out_specs.append(pl.BlockSpec((None, tm // r, r * C_COLS), lambda bi, i: (bi, i, 0)))
    return pl.pallas_call(
        functools.partial(_proj_kernel, dils=dils),
        grid=(b, nt),
        in_specs=[tok(d), pl.BlockSpec((1, d), lambda bi, i: (0, 0)), full(w_nat), full(w_avt), full(w_c)],
        out_specs=out_specs,
        out_shape=out_shapes,
        scratch_shapes=[pltpu.VMEM((C_COLS // LANES, tm, LANES), F32)],
        compiler_params=_params("parallel", "parallel"),
        name="qkv_proj",
    )(x3, g.reshape(1, d), w_nat, w_avt, w_c)


def _stack_masked_q(q):
    lane = lax.broadcasted_iota(jnp.int32, q.shape, 1)
    zero = jnp.zeros_like(q)
    return jnp.concatenate([jnp.where(lane < HEAD_DIM, q, zero),
                            jnp.where(lane >= HEAD_DIM, q, zero)], axis=0)


def _nt_dot(a, b):
    return lax.dot_general(a, b, (((1,), (1,)), ((), ())), preferred_element_type=F32)


def _pick_head_lanes(lo, hi):
    lane = lax.broadcasted_iota(jnp.int32, lo.shape, 1)
    return jnp.where(lane < HEAD_DIM, lo, hi)


def _attn_a_kernel(tbl_ref, lam_ref, q_ref, k_ref, vt_ref, g_ref, o_ref,
                   qz_ref, m_ref, acc_ref, bias_ref, s0_ref, s1_ref, cm0_ref, cm1_ref,
                   p0_ref, p1_ref, al0_ref, al1_ref, *, t, sw, nn, nblk, lam_init):
    h = pl.program_id(0)
    bi = pl.program_id(1)
    i = pl.program_id(2)
    nb = T5_BUCKETS // 2
    max_rel = (nn + 1) * t
    sat = _t5_change_points(max_rel)[-1][1]
    slots = ((s0_ref, cm0_ref, p0_ref, al0_ref), (s1_ref, cm1_ref, p1_ref, al1_ref))

    @pl.when(jnp.logical_and(bi == 0, i == 0))
    def _():
        kk = lax.broadcasted_iota(jnp.int32, (t, t), 0)
        qq = lax.broadcasted_iota(jnp.int32, (t, t), 1)
        bias_ref[0] = jnp.full((t, t), tbl_ref[sat, h] * LOG2E, F32)
        bias_ref[2 * nn + 2] = jnp.full((t, t), tbl_ref[nb + sat, h] * LOG2E, F32)
        for d in range(-nn, nn + 1):
            bias_ref[d + nn + 1] = _t5_bias(d * t + kk - qq, tbl_ref, h, max_rel) * LOG2E

    qz_ref[...] = _stack_masked_q(q_ref[...])
    m_ref[...] = jnp.full_like(m_ref, -jnp.inf)
    acc_ref[...] = jnp.zeros_like(acc_ref)

    n_near = 2 * nn + 1
    w0 = jnp.clip(i - nn, 0, nblk - n_near)
    c_left = tbl_ref[sat, h] * LOG2E
    c_right = tbl_ref[nb + sat, h] * LOG2E

    def block_of(pos):
        if pos < n_near:
            return w0 + pos, None
        j = jnp.where(pos - n_near < w0, pos - n_near, pos)
        return j, jnp.where(j < i, c_left, c_right)

    order = [block_of(pos) for pos in range(nblk)]

    def scores(pos, c):
        j, shift = order[pos]
        s_ref, cm_ref, _, _ = slots[pos % 2]
        cols = slice(c * sw, (c + 1) * sw)
        kb = k_ref[pl.ds(pl.multiple_of(j * t, t), t), :]
        s = _nt_dot(kb, qz_ref[cols, :])
        if shift is None:
            tcols = slice((c * sw) % t, (c * sw) % t + sw)
            s = s + bias_ref[jnp.clip(j - i, -(nn + 1), nn + 1) + nn + 1, :, tcols]
        s_ref[:, cols] = s
        cmax = jnp.max(s, axis=0, keepdims=True)
        cm_ref[:, cols] = cmax if shift is None else cmax + shift

    def softmax(pos, c):
        _, shift = order[pos]
        s_ref, cm_ref, p_ref, al_ref = slots[pos % 2]
        cols = slice(c * sw, (c + 1) * sw)
        m_prev = m_ref[:, cols]
        m_new = jnp.maximum(m_prev, cm_ref[:, cols])
        al_ref[:, cols] = jnp.exp2(m_prev - m_new)
        sub = m_new if shift is None else m_new - shift
        p_ref[:, cols] = jnp.exp2(s_ref[:, cols] - sub).astype(BF16)
        m_ref[:, cols] = m_new

    def values(pos, c):
        j, _ = order[pos]
        _, _, p_ref, al_ref = slots[pos % 2]
        cols = slice(c * sw, (c + 1) * sw)
        acc_ref[:, cols] = al_ref[:, cols] * acc_ref[:, cols] + jnp.dot(
            vt_ref[j], p_ref[:, cols], preferred_element_type=F32)

    for tau in range(nblk + 2):
        for c in range(2 * t // sw):
            if tau < nblk:
                scores(tau, c)
            if 0 <= tau - 1 < nblk:
                softmax(tau - 1, c)
            if 0 <= tau - 2 < nblk:
                values(tau - 2, c)

    lam = (jnp.exp(jnp.sum(lam_ref[0:1, :] * lam_ref[1:2, :], axis=-1, keepdims=True))
           - jnp.exp(jnp.sum(lam_ref[2:3, :] * lam_ref[3:4, :], axis=-1, keepdims=True)) + lam_init)
    o = acc_ref[0:PAIR_W, :] / acc_ref[PAIR_W:PAIR_W + 1, :]
    o = o[:, :t] - lam * o[:, t:]
    o = o * lax.rsqrt(jnp.mean(o * o, axis=0, keepdims=True) + EPS) * g_ref[...] * (1.0 - lam_init)
    o_ref[...] = o.T.astype(BF16)


def _attn_a(nat, avt, t5_table, lam, subln_g, *, t, lam_init):
    b, s, _ = nat.shape
    nblk = s // t
    sat_n = _t5_saturation(s)
    nn = -(-(sat_n - 1) // t)
    assert nblk >= 2 * nn + 1, "the near-diagonal window must fit in the sequence"
    slot_scratch = [pltpu.VMEM((t, 2 * t), F32)] * 2 + [pltpu.VMEM((1, 2 * t), F32)] * 2 \
        + [pltpu.VMEM((t, 2 * t), BF16)] * 2 + [pltpu.VMEM((1, 2 * t), F32)] * 2
    return pl.pallas_call(
        functools.partial(_attn_a_kernel, t=t, sw=min(MXU_TILE, t), nn=nn, nblk=nblk,
                          lam_init=lam_init),
        grid=(A_HEADS, b, nblk),
        in_specs=[
            pl.BlockSpec(memory_space=pltpu.SMEM),
            pl.BlockSpec((4, HEAD_DIM), lambda h, bi, i: (0, 0)),
            pl.BlockSpec((None, t, PAIR_W), lambda h, bi, i: (bi, i, NAT_BLK["aq"] + h)),
            pl.BlockSpec((None, s, PAIR_W), lambda h, bi, i: (bi, 0, NAT_BLK["ak"] + h)),
            pl.BlockSpec((None, nblk, AVT_ROWS, t), lambda h, bi, i: (bi, 0, h, 0)),
            pl.BlockSpec((PAIR_W, 1), lambda h, bi, i: (0, 0)),
        ],
        out_specs=pl.BlockSpec((None, t, PAIR_W), lambda h, bi, i: (bi, i, h)),
        out_shape=jax.ShapeDtypeStruct((b, s, A_W), BF16),
        scratch_shapes=[
            pltpu.VMEM((2 * t, PAIR_W), BF16),
            pltpu.VMEM((1, 2 * t), F32),
            pltpu.VMEM((AVT_ROWS, 2 * t), F32),
            pltpu.VMEM((2 * nn + 3, t, t), F32),
        ] + slot_scratch,
        compiler_params=_params("arbitrary", "arbitrary", "arbitrary"),
        name="attn_diff",
    )(t5_table, lam, nat, nat, avt, subln_g.reshape(PAIR_W, 1))


def _attn_b_kernel(q_ref, k_ref, v_ref, bm_ref, o_ref, *, tq, tk, s_len):
    g = pl.program_id(2)
    start = pl.multiple_of(jnp.clip(g * tq - (NA_ROWS_MAX // 2) * GRID_W, 0, s_len - tk), GRID_W)
    qz = _stack_masked_q(q_ref[...])
    sc = _nt_dot(qz, k_ref[pl.ds(start, tk), :]) + bm_ref[...]
    m = jnp.max(sc, axis=-1, keepdims=True)
    p = jnp.exp(sc - m)
    l = jnp.sum(p, axis=-1, keepdims=True)
    o = jnp.dot(p.astype(BF16), v_ref[pl.ds(start, tk), :], preferred_element_type=F32) / l
    o_ref[...] = _pick_head_lanes(o[:tq], o[tq:]).astype(BF16)


def _attn_b(nat, bm, *, qrows):
    b, s, _ = nat.shape
    tq = qrows * GRID_W
    tk = (qrows + NA_ROWS_MAX) * GRID_W
    ngrp = s // tq
    npair = B_HEADS // 2

    def variant(g):
        return jnp.where(g > 0, 1, 0) + jnp.where(g == ngrp - 1, 1, 0)

    return pl.pallas_call(
        functools.partial(_attn_b_kernel, tq=tq, tk=tk, s_len=s),
        grid=(b, npair, ngrp),
        in_specs=[
            pl.BlockSpec((None, tq, PAIR_W), lambda bi, pr, g: (bi, g, NAT_BLK["bq"] + pr)),
            pl.BlockSpec((None, s, PAIR_W), lambda bi, pr, g: (bi, 0, NAT_BLK["bk"] + pr)),
            pl.BlockSpec((None, s, PAIR_W), lambda bi, pr, g: (bi, 0, NAT_BLK["bv"] + pr)),
            pl.BlockSpec((None, None, 2 * tq, tk), lambda bi, pr, g: (pr, variant(g), 0, 0)),
        ],
        out_specs=pl.BlockSpec((None, tq, PAIR_W), lambda bi, pr, g: (bi, g, pr)),
        out_shape=jax.ShapeDtypeStruct((b, s, B_W), BF16),
        compiler_params=_params("parallel", "parallel", "arbitrary"),
        name="attn_nbr",
    )(nat, nat, nat, bm)


def _b_bias_tiles(rpb, qrows, rows):
    kh = min(NA_ROWS_MAX, rows)
    kw = min(NA_COLS, GRID_W)
    krows = qrows + NA_ROWS_MAX
    ngrp = rows // qrows
    n_rel_r, n_rel_c = 2 * NA_ROWS_MAX - 1, 2 * NA_COLS - 1
    qc = np.arange(GRID_W)[:, None]
    kc = np.arange(GRID_W)[None, :]
    cstart = np.clip(qc - kw // 2, 0, GRID_W - kw)
    col_ok = (kc >= cstart) & (kc < cstart + kw)
    rel_c = kc - qc + (NA_COLS - 1)
    onehot_c = (rel_c[None] == np.arange(n_rel_c)[:, None, None]).astype(np.float32)
    blocks = jnp.einsum("hac,cqk->haqk", rpb.astype(F32), onehot_c, precision=lax.Precision.HIGHEST)
    blocks = jnp.where(col_ok[None, None], blocks, MASKED)
    blocks = jnp.concatenate([blocks, jnp.full((B_HEADS, 1, GRID_W, GRID_W), MASKED, F32)], axis=1)
    block_idx = []
    for g in (0, 1, ngrp - 1):
        r_abs = g * qrows + np.arange(qrows)[:, None]
        k0 = int(np.clip(g * qrows - NA_ROWS_MAX // 2, 0, rows - krows))
        kr_abs = k0 + np.arange(krows)[None, :]
        rstart = np.clip(r_abs - kh // 2, 0, rows - kh)
        row_ok = (kr_abs >= rstart) & (kr_abs < rstart + kh)
        block_idx.append(np.where(row_ok, kr_abs - r_abs + (NA_ROWS_MAX - 1), n_rel_r))
    block_idx = np.stack(block_idx)
    tiles = jnp.take(blocks, block_idx.reshape(-1), axis=1)
    tiles = tiles.reshape(B_HEADS // 2, 2, 3, qrows, krows, GRID_W, GRID_W)
    tiles = tiles.transpose(0, 2, 1, 3, 5, 4, 6)
    return tiles.reshape(B_HEADS // 2, 3, 2 * qrows * GRID_W, krows * GRID_W)


def _attn_c_kernel(tbl_ref, q_ref, k_ref, v_ref, o_ref, lse_ref, bm_ref, *, r, tq, half, s_r):
    tk = tq + 2 * half
    nblk = s_r // tq
    pr = pl.program_id(0)

    @pl.when(jnp.logical_and(pl.program_id(1) == 0, pl.program_id(2) == 0))
    def _():
        qq = lax.broadcasted_iota(jnp.int32, (tq, tk), 0)
        kk = lax.broadcasted_iota(jnp.int32, (tq, tk), 1)
        for var, off in enumerate((0, -half, -2 * half)):
            jj = kk - qq + off
            valid = jnp.abs(jj) <= half
            rel = r * jnp.clip(jj, -half, half)
            for hh in range(2):
                bias = _t5_bias(rel, tbl_ref, A_HEADS + 2 * pr + hh, r * half)
                bm_ref[var, hh * tq:(hh + 1) * tq, :] = jnp.where(valid, bias, MASKED)

    def body(i, carry):
        q0 = pl.multiple_of(i * tq, tq)
        ws = pl.multiple_of(jnp.clip(i * tq - half, 0, s_r - tk), half)
        var = jnp.where(i > 0, 1, 0) + jnp.where(i == nblk - 1, 1, 0)
        qz = _stack_masked_q(q_ref[pl.ds(q0, tq), :])
        sc = _nt_dot(qz, k_ref[pl.ds(ws, tk), :]) + bm_ref[var]
        m = jnp.max(sc, axis=-1, keepdims=True)
        p = jnp.exp(sc - m)
        l = jnp.sum(p, axis=-1, keepdims=True)
        o = jnp.dot(p.astype(BF16), v_ref[pl.ds(ws, tk), :], preferred_element_type=F32) / l
        lse = jnp.broadcast_to(m + jnp.log(l), o.shape)
        o_ref[pl.ds(q0, tq), :] = _pick_head_lanes(o[:tq], o[tq:]).astype(BF16)
        lse_ref[pl.ds(q0, tq), :] = _pick_head_lanes(lse[:tq], lse[tq:])
        return carry

    lax.fori_loop(0, nblk, body, 0)


def _attn_c(c_r, t5_table, *, r, tq, half):
    b, s_r, _ = c_r.shape
    npair = C_HEADS // 2
    ncb = C_COLS // PAIR_W
    nob = C_W // PAIR_W

    def in_spec(name):
        return pl.BlockSpec((None, s_r, PAIR_W),
                            lambda pr, bi, rho: (bi, 0, rho * ncb + C_BLK[name] + pr))

    out_spec = pl.BlockSpec((None, s_r, PAIR_W), lambda pr, bi, rho: (bi, 0, rho * nob + pr))
    o, lse = pl.pallas_call(
        functools.partial(_attn_c_kernel, r=r, tq=tq, half=half, s_r=s_r),
        grid=(npair, b, r),
        in_specs=[pl.BlockSpec(memory_space=pltpu.SMEM), in_spec("cq"), in_spec("ck"), in_spec("cv")],
        out_specs=[out_spec, out_spec],
        out_shape=[jax.ShapeDtypeStruct((b, s_r, r * C_W), BF16),
                   jax.ShapeDtypeStruct((b, s_r, r * C_W), F32)],
        scratch_shapes=[pltpu.VMEM((3, 2 * tq, tq + 2 * half), F32)],
        compiler_params=_params("arbitrary", "arbitrary", "arbitrary"),
        name=f"attn_dil{r}",
    )(t5_table, c_r, c_r, c_r)
    return o.reshape(b * s_r, r * C_W), lse.reshape(b * s_r, r * C_W)


def _merge_kernel(x_ref, g_ref, oa_ref, ob_ref, *rest, dils):
    nc = len(dils)
    oc_refs, ls_refs = rest[:nc], rest[nc:2 * nc]
    wg_ref, wa_ref, wb_ref, wc_ref, wo_ref, o_ref, tok_ref = rest[2 * nc:]
    x = x_ref[...]
    tm, d = x.shape
    h = _rms(x, g_ref[...]).astype(BF16)

    def token_order(ref, r):
        if r == 1:
            return ref[...].astype(F32)
        for rho in range(r):
            for cb in range(C_W // LANES):
                lo = rho * C_W + cb * LANES
                tok_ref[cb, pl.ds(rho, tm // r, stride=r), :] = ref[:, lo:lo + LANES].astype(F32)
        return jnp.concatenate([tok_ref[cb] for cb in range(C_W // LANES)], axis=1)

    lses = [token_order(ref, r) for ref, r in zip(ls_refs, dils)]
    mx = functools.reduce(jnp.maximum, lses)
    es = [jnp.exp(ls - mx) for ls in lses]
    num = sum(e * token_order(ref, r) for e, ref, r in zip(es, oc_refs, dils))
    oc = num / sum(es)

    merged = jnp.zeros(x.shape, F32)
    for n, (br, w_ref) in enumerate(((oa_ref[...], wa_ref), (ob_ref[...], wb_ref),
                                     (oc.astype(BF16), wc_ref))):
        gate = jnp.dot(h, wg_ref[:, n * d:(n + 1) * d], preferred_element_type=F32)
        merged = merged + jax.nn.sigmoid(gate) * jnp.dot(br, w_ref[...], preferred_element_type=F32)
    o_ref[...] = x + jnp.dot(merged.astype(BF16), wo_ref[...], preferred_element_type=F32)


def _merge(x2, g, oa, ob, ocs, lses, wg, wa, wb, wc, wo, *, tm, dils):
    n, d = x2.shape
    tok = lambda w: pl.BlockSpec((tm, w), lambda i: (i, 0))
    dil = lambda r: pl.BlockSpec((tm // r, r * C_W), lambda i: (i, 0))
    full = lambda a: pl.BlockSpec(a.shape, lambda i: (0, 0))
    return pl.pallas_call(
        functools.partial(_merge_kernel, dils=dils),
        grid=(n // tm,),
        in_specs=[tok(d), pl.BlockSpec((1, d), lambda i: (0, 0)), tok(A_W), tok(B_W)]
                 + [dil(r) for r in dils] + [dil(r) for r in dils]
                 + [full(wg), full(wa), full(wb), full(wc), full(wo)],
        out_specs=tok(d),
        out_shape=jax.ShapeDtypeStruct((n, d), F32),
        scratch_shapes=[pltpu.VMEM((C_W // LANES, tm, LANES), F32)],
        compiler_params=_params("parallel"),
        name="gated_merge",
    )(x2, g.reshape(1, d), oa, ob, *ocs, *lses, wg, wa, wb, wc, wo)


def _tiles(n_tok, s, d_ff):
    return dict(
        ffn_tm=min(1024, n_tok), ffn_tf=256 if d_ff % 256 == 0 else d_ff,
        a_t=min(512, s),
        b_qrows=4,
        c_tq=128,
        merge_tm=min(256, n_tok),
    )


def kernel(x, g_ff1, w1_ff1, w3_ff1, w2_ff1, g_mix, w_in, lam_q1, lam_k1, lam_q2, lam_k2, subln_g,
           na_rpb, t5_table, w_br_a, w_br_b, w_br_c, w_o, g_ff2, w1_ff2, w3_ff2, w2_ff2, g_final):
    b, s, d = x.shape
    depth = w_in.shape[0]
    n_tok = b * s
    rows = s // GRID_W
    tl = _tiles(n_tok, s, w1_ff1.shape[2])
    bf = lambda w: w.astype(BF16)
    c_cfgs = [(w // (2 * r), r) for (w, r) in C_CONFIGS]
    dils = tuple(r for _, r in c_cfgs)
    t5_table = t5_table.astype(F32)

    col = lambda i0, i1: slice(512 * i0, 512 * i1)

    x2 = x.reshape(n_tok, d)
    for l in range(depth):
        x2 = _ffn(x2, g_ff1[l], bf(w1_ff1[l]), bf(w3_ff1[l]), bf(w2_ff1[l]), g_final,
                  final_norm=False, tm=tl["ffn_tm"], tf=tl["ffn_tf"])

        w = w_in[l]
        w_nat = bf(jnp.concatenate([w[:, col(0, 2)], w[:, col(3, 6)]], axis=1))
        w_avt = bf(w[:, col(2, 3)].T)
        w_c = bf(w[:, col(6, 9)])
        nat, avt, *c_rs = _proj(x2.reshape(b, s, d), g_mix[l], w_nat, w_avt, w_c,
                                tm=tl["a_t"], dils=dils)

        lam_init = 0.8 - 0.6 * math.exp(-0.3 * l)
        lam_vecs = jnp.stack([lam_q1[l], lam_k1[l], lam_q2[l], lam_k2[l]])
        o_a = _attn_a(nat, avt, t5_table, lam_vecs, subln_g[l], t=tl["a_t"], lam_init=lam_init)
        o_b = _attn_b(nat, _b_bias_tiles(na_rpb[l], tl["b_qrows"], rows), qrows=tl["b_qrows"])
        c_out = [_attn_c(c_r, t5_table, r=r, tq=tl["c_tq"], half=half)
                 for c_r, (half, r) in zip(c_rs, c_cfgs)]

        x2 = _merge(x2, g_mix[l], o_a.reshape(n_tok, A_W), o_b.reshape(n_tok, B_W),
                    [o for o, _ in c_out], [ls for _, ls in c_out],
                    bf(w[:, col(9, 15)]), bf(w_br_a[l]), bf(w_br_b[l]), bf(w_br_c[l]), bf(w_o[l]),
                    tm=tl["merge_tm"], dils=dils)

        x2 = _ffn(x2, g_ff2[l], bf(w1_ff2[l]), bf(w3_ff2[l]), bf(w2_ff2[l]), g_final,
                  final_norm=(l == depth - 1), tm=tl["ffn_tm"], tf=tl["ffn_tf"])
    return x2.reshape(b, s, d)
```

```python
import functools
import math

import numpy as np
import jax
import jax.numpy as jnp
from jax import lax
from jax.experimental import pallas as pl
from jax.experimental.pallas import tpu as pltpu

HEAD_DIM = 64
A_HEADS = 4
B_HEADS = 8
C_HEADS = 8
GRID_W = 64
NA_ROWS_MAX = 8
NA_COLS = 16
C_CONFIGS = ((128, 1), (512, 4), (2048, 16))
T5_BUCKETS = 32
T5_MAX_DIST = 1024
EPS = 1e-6

LANES = 128
MXU_TILE = 256
C_UNROLL = 4
VMEM_LIMIT_BYTES = 56 * 1024 * 1024

MASKED = -1e30
BF16 = jnp.bfloat16
F32 = jnp.float32

PAIR_W = 2 * HEAD_DIM
A_W = A_HEADS * PAIR_W
B_W = B_HEADS * HEAD_DIM
C_W = C_HEADS * HEAD_DIM
GROUP_W = 512
LOG2E = math.log2(math.e)
QK_SCALE = HEAD_DIM ** -0.5
NAT_BLK = {"aq": 0, "ak": 4, "bq": 8, "bk": 12, "bv": 16}
NAT_SCALE = (QK_SCALE * LOG2E, 1.0, QK_SCALE, 1.0, 1.0)
NAT_COLS = len(NAT_SCALE) * GROUP_W
C_BLK = {"cq": 0, "ck": 4, "cv": 8}
C_SCALE = (QK_SCALE, 1.0, 1.0)
C_COLS = len(C_SCALE) * GROUP_W
AVT_ROWS = PAIR_W + 16


def _params(*sem):
    return pltpu.CompilerParams(dimension_semantics=sem, vmem_limit_bytes=VMEM_LIMIT_BYTES)


def _rms(x, g):
    return x * lax.rsqrt(jnp.mean(x * x, axis=-1, keepdims=True) + EPS) * g


def _t5_abs_bucket_np(n):
    nb = T5_BUCKETS // 2
    max_exact = nb // 2
    n = np.asarray(n, np.int64)
    x = np.log(np.maximum(n, 1) / max_exact) / math.log(T5_MAX_DIST / max_exact) * (nb - max_exact)
    interior = (n > max_exact) & (n < T5_MAX_DIST)
    assert np.all(np.abs(x[interior] - np.round(x[interior])) > 1e-6)
    large = np.minimum(max_exact + np.floor(x + 1e-9).astype(np.int64), nb - 1)
    return np.where(n < max_exact, n, large)


def _t5_change_points(max_n):
    b = _t5_abs_bucket_np(np.arange(max_n + 1))
    return [(0, int(b[0]))] + [(n, int(b[n])) for n in range(1, max_n + 1) if b[n] != b[n - 1]]


def _t5_saturation(s):
    return _t5_change_points(s - 1)[-1][0]


def _t5_bias(rel, tbl_ref, head, max_n):
    nb = T5_BUCKETS // 2
    cps = _t5_change_points(max_n)
    n = jnp.abs(rel)
    neg = jnp.full(rel.shape, tbl_ref[cps[-1][1], head], F32)
    pos = jnp.full(rel.shape, tbl_ref[nb + cps[-1][1], head], F32)
    for (n0, bkt), (n1, _) in reversed(list(zip(cps[:-1], cps[1:]))):
        below = n < n1
        neg = jnp.where(below, tbl_ref[bkt, head], neg)
        pos = jnp.where(below, tbl_ref[nb + bkt, head], pos)
    return jnp.where(rel > 0, pos, neg)


def _ffn_kernel(x_ref, g_ref, w1_ref, w3_ref, w2_ref, gf_ref, o_ref, acc_ref, *, final_norm, tf):
    x = x_ref[...]
    h = _rms(x, g_ref[...]).astype(BF16)
    for c in range(w1_ref.shape[1] // tf):
        cols = slice(c * tf, (c + 1) * tf)
        a = jnp.dot(h, w1_ref[:, cols], preferred_element_type=F32)
        b = jnp.dot(h, w3_ref[:, cols], preferred_element_type=F32)
        u = (a * jax.nn.sigmoid(a) * b).astype(BF16)
        part = jnp.dot(u, w2_ref[cols, :], preferred_element_type=F32)
        if c == 0:
            acc_ref[...] = part
        else:
            acc_ref[...] += part
    y = x + 0.5 * acc_ref[...]
    if final_norm:
        y = _rms(y, gf_ref[...])
    o_ref[...] = y


def _resident(a):
    return pl.BlockSpec(a.shape, lambda *_: (0,) * a.ndim, pipeline_mode=pl.Buffered(1))


def _ffn(x2, g, w1, w3, w2, g_final, *, final_norm, tm, tf):
    n, d = x2.shape
    return pl.pallas_call(
        functools.partial(_ffn_kernel, final_norm=final_norm, tf=tf),
        grid=(n // tm,),
        in_specs=[
            pl.BlockSpec((tm, d), lambda i: (i, 0)),
            pl.BlockSpec((1, d), lambda i: (0, 0)),
            _resident(w1), _resident(w3), _resident(w2),
            pl.BlockSpec((1, d), lambda i: (0, 0)),
        ],
        out_specs=pl.BlockSpec((tm, d), lambda i: (i, 0)),
        out_shape=jax.ShapeDtypeStruct((n, d), F32),
        scratch_shapes=[pltpu.VMEM((tm, d), F32)],
        compiler_params=_params("parallel"),
        name="ffn",
    )(x2, g.reshape(1, d), w1, w3, w2, g_final.reshape(1, d))


def _proj_kernel(x_ref, g_ref, wn_ref, wvt_ref, wc_ref, nat_ref, avt_ref, *rest, dils):
    c_refs, cs_ref = rest[:-1], rest[-1]
    tm = x_ref.shape[0]
    tn = GROUP_W
    h = _rms(x_ref[...], g_ref[...]).astype(BF16)
    for c, scale in enumerate(NAT_SCALE):
        res = jnp.dot(h, wn_ref[:, c * tn:(c + 1) * tn], preferred_element_type=F32)
        nat_ref[:, c * tn:(c + 1) * tn] = (res if scale == 1.0 else res * scale).astype(BF16)
    vt = lax.dot_general(wvt_ref[...], h, (((1,), (1,)), ((), ())),
                         preferred_element_type=F32).astype(BF16)
    for hd in range(A_HEADS):
        avt_ref[hd * AVT_ROWS:hd * AVT_ROWS + PAIR_W, :] = vt[hd * PAIR_W:(hd + 1) * PAIR_W]
        avt_ref[hd * AVT_ROWS + PAIR_W:(hd + 1) * AVT_ROWS, :] = jnp.ones((AVT_ROWS - PAIR_W, tm), BF16)
    per = tn // LANES
    for c, scale in enumerate(C_SCALE):
        res = jnp.dot(h, wc_ref[:, c * tn:(c + 1) * tn], preferred_element_type=F32)
        res = res if scale == 1.0 else res * scale
        for k in range(per):
            cs_ref[c * per + k] = res[:, k * LANES:(k + 1) * LANES]
    for c_ref, r in zip(c_refs, dils):
        for rho in range(r):
            for cb in range(C_COLS // LANES):
                rows = cs_ref[cb] if r == 1 else cs_ref[cb, pl.ds(rho, tm // r, stride=r), :]
                c_ref[:, rho * C_COLS + cb * LANES:rho * C_COLS + (cb + 1) * LANES] = rows.astype(BF16)


def _proj(x3, g, w_nat, w_avt, w_c, *, tm, dils):
    b, s, d = x3.shape
    nt = s // tm
    tok = lambda w: pl.BlockSpec((None, tm, w), lambda bi, i: (bi, i, 0))
    full = _resident
    out_shapes = [jax.ShapeDtypeStruct((b, s, NAT_COLS), BF16),
                  jax.ShapeDtypeStruct((b, nt, A_HEADS * AVT_ROWS, tm), BF16)]
    out_specs = [tok(NAT_COLS),
                 pl.BlockSpec((None, None, A_HEADS * AVT_ROWS, tm), lambda bi, i: (bi, i, 0, 0))]
    for r in dils:
        out_shapes.append(jax.ShapeDtypeStruct((b, s // r, r * C_COLS), BF16))
        out_specs.append(pl.BlockSpec((None, tm // r, r * C_COLS), lambda bi, i: (bi, i, 0)))
    return pl.pallas_call(
        functools.partial(_proj_kernel, dils=dils),
        grid=(b, nt),
        in_specs=[tok(d), pl.BlockSpec((1, d), lambda bi, i: (0, 0)), full(w_nat), full(w_avt), full(w_c)],
        out_specs=out_specs,
        out_shape=out_shapes,
        scratch_shapes=[pltpu.VMEM((C_COLS // LANES, tm, LANES), F32)],
        compiler_params=_params("parallel", "parallel"),
        name="qkv_proj",
    )(x3, g.reshape(1, d), w_nat, w_avt, w_c)


def _stack_masked_q(q):
    lane = lax.broadcasted_iota(jnp.int32, q.shape, 1)
    zero = jnp.zeros_like(q)
    return jnp.concatenate([jnp.where(lane < HEAD_DIM, q, zero),
                            jnp.where(lane >= HEAD_DIM, q, zero)], axis=0)


def _nt_dot(a, b):
    return lax.dot_general(a, b, (((1,), (1,)), ((), ())), preferred_element_type=F32)


def _pick_head_lanes(lo, hi):
    lane = lax.broadcasted_iota(jnp.int32, lo.shape, 1)
    return jnp.where(lane < HEAD_DIM, lo, hi)


def _mask_head_lanes(q, hh):
    lane = lax.broadcasted_iota(jnp.int32, q.shape, 1)
    keep = lane < HEAD_DIM if hh == 0 else lane >= HEAD_DIM
    return jnp.where(keep, q, jnp.zeros_like(q))


def _with_ones(v):
    return jnp.concatenate([v, jnp.ones_like(v)], axis=1)


def _softmax_pv(sc, v_ones):
    m = jnp.max(sc, axis=-1, keepdims=True)
    p = jnp.exp(sc - m).astype(BF16)
    res = jnp.dot(p, v_ones, preferred_element_type=F32)
    l = res[:, PAIR_W:]
    return res[:, :PAIR_W] / l, m + jnp.log(l)


def _attn_a_kernel(tbl_ref, lam_ref, q_ref, k_ref, vt_ref, g_ref, o_ref,
                   qz_ref, m_ref, acc_ref, bias_ref, s0_ref, s1_ref, cm0_ref, cm1_ref,
                   p0_ref, p1_ref, al0_ref, al1_ref, *, t, sw, nn, nblk, lam_init):
    h = pl.program_id(0)
    bi = pl.program_id(1)
    i = pl.program_id(2)
    nb = T5_BUCKETS // 2
    max_rel = (nn + 1) * t
    sat = _t5_change_points(max_rel)[-1][1]
    slots = ((s0_ref, cm0_ref, p0_ref, al0_ref), (s1_ref, cm1_ref, p1_ref, al1_ref))

    @pl.when(jnp.logical_and(bi == 0, i == 0))
    def _():
        kk = lax.broadcasted_iota(jnp.int32, (t, t), 0)
        qq = lax.broadcasted_iota(jnp.int32, (t, t), 1)
        bias_ref[0] = jnp.full((t, t), tbl_ref[sat, h] * LOG2E, F32)
        bias_ref[2 * nn + 2] = jnp.full((t, t), tbl_ref[nb + sat, h] * LOG2E, F32)
        for d in range(-nn, nn + 1):
            bias_ref[d + nn + 1] = _t5_bias(d * t + kk - qq, tbl_ref, h, max_rel) * LOG2E

    qz_ref[...] = _stack_masked_q(q_ref[...])
    m_ref[...] = jnp.full_like(m_ref, -jnp.inf)
    acc_ref[...] = jnp.zeros_like(acc_ref)

    n_near = 2 * nn + 1
    w0 = jnp.clip(i - nn, 0, nblk - n_near)
    c_left = tbl_ref[sat, h] * LOG2E
    c_right = tbl_ref[nb + sat, h] * LOG2E

    def block_of(pos):
        if pos < n_near:
            return w0 + pos, None
        j = jnp.where(pos - n_near < w0, pos - n_near, pos)
        return j, jnp.where(j < i, c_left, c_right)

    order = [block_of(pos) for pos in range(nblk)]

    def scores(pos, c):
        j, shift = order[pos]
        s_ref, cm_ref, _, _ = slots[pos % 2]
        cols = slice(c * sw, (c + 1) * sw)
        kb = k_ref[pl.ds(pl.multiple_of(j * t, t), t), :]
        s = _nt_dot(kb, qz_ref[cols, :])
        if shift is None:
            tcols = slice((c * sw) % t, (c * sw) % t + sw)
            s = s + bias_ref[jnp.clip(j - i, -(nn + 1), nn + 1) + nn + 1, :, tcols]
        s_ref[:, cols] = s
        cmax = jnp.max(s, axis=0, keepdims=True)
        cm_ref[:, cols] = cmax if shift is None else cmax + shift

    def softmax(pos, c):
        _, shift = order[pos]
        s_ref, cm_ref, p_ref, al_ref = slots[pos % 2]
        cols = slice(c * sw, (c + 1) * sw)
        m_prev = m_ref[:, cols]
        m_new = jnp.maximum(m_prev, cm_ref[:, cols])
        al_ref[:, cols] = jnp.exp2(m_prev - m_new)
        sub = m_new if shift is None else m_new - shift
        p_ref[:, cols] = jnp.exp2(s_ref[:, cols] - sub).astype(BF16)
        m_ref[:, cols] = m_new

    def values(pos, c):
        j, _ = order[pos]
        _, _, p_ref, al_ref = slots[pos % 2]
        cols = slice(c * sw, (c + 1) * sw)
        acc_ref[:, cols] = al_ref[:, cols] * acc_ref[:, cols] + jnp.dot(
            vt_ref[j], p_ref[:, cols], preferred_element_type=F32)

    for tau in range(nblk + 2):
        for c in range(2 * t // sw):
            if tau < nblk:
                scores(tau, c)
            if 0 <= tau - 1 < nblk:
                softmax(tau - 1, c)
            if 0 <= tau - 2 < nblk:
                values(tau - 2, c)

    lam = (jnp.exp(jnp.sum(lam_ref[0:1, :] * lam_ref[1:2, :], axis=-1, keepdims=True))
           - jnp.exp(jnp.sum(lam_ref[2:3, :] * lam_ref[3:4, :], axis=-1, keepdims=True)) + lam_init)
    o = acc_ref[0:PAIR_W, :] / acc_ref[PAIR_W:PAIR_W + 1, :]
    o = o[:, :t] - lam * o[:, t:]
    o = o * lax.rsqrt(jnp.mean(o * o, axis=0, keepdims=True) + EPS) * g_ref[...] * (1.0 - lam_init)
    o_ref[...] = o.T.astype(BF16)


def _attn_a(nat, avt, t5_table, lam, subln_g, *, t, lam_init):
    b, s, _ = nat.shape
    nblk = s // t
    sat_n = _t5_saturation(s)
    nn = -(-(sat_n - 1) // t)
    assert nblk >= 2 * nn + 1, "the near-diagonal window must fit in the sequence"
    slot_scratch = [pltpu.VMEM((t, 2 * t), F32)] * 2 + [pltpu.VMEM((1, 2 * t), F32)] * 2 \
        + [pltpu.VMEM((t, 2 * t), BF16)] * 2 + [pltpu.VMEM((1, 2 * t), F32)] * 2
    return pl.pallas_call(
        functools.partial(_attn_a_kernel, t=t, sw=min(MXU_TILE, t), nn=nn, nblk=nblk,
                          lam_init=lam_init),
        grid=(A_HEADS, b, nblk),
        in_specs=[
            pl.BlockSpec(memory_space=pltpu.SMEM),
            pl.BlockSpec((4, HEAD_DIM), lambda h, bi, i: (0, 0)),
            pl.BlockSpec((None, t, PAIR_W), lambda h, bi, i: (bi, i, NAT_BLK["aq"] + h)),
            pl.BlockSpec((None, s, PAIR_W), lambda h, bi, i: (bi, 0, NAT_BLK["ak"] + h)),
            pl.BlockSpec((None, nblk, AVT_ROWS, t), lambda h, bi, i: (bi, 0, h, 0)),
            pl.BlockSpec((PAIR_W, 1), lambda h, bi, i: (0, 0)),
        ],
        out_specs=pl.BlockSpec((None, t, PAIR_W), lambda h, bi, i: (bi, i, h)),
        out_shape=jax.ShapeDtypeStruct((b, s, A_W), BF16),
        scratch_shapes=[
            pltpu.VMEM((2 * t, PAIR_W), BF16),
            pltpu.VMEM((1, 2 * t), F32),
            pltpu.VMEM((AVT_ROWS, 2 * t), F32),
            pltpu.VMEM((2 * nn + 3, t, t), F32),
        ] + slot_scratch,
        compiler_params=_params("arbitrary", "arbitrary", "arbitrary"),
        name="attn_diff",
    )(t5_table, lam, nat, nat, avt, subln_g.reshape(PAIR_W, 1))


def _attn_b_kernel(q_ref, k_ref, v_ref, bm_ref, o_ref, *, tq, tk, s_len):
    g = pl.program_id(2)
    start = pl.multiple_of(jnp.clip(g * tq - (NA_ROWS_MAX // 2) * GRID_W, 0, s_len - tk), GRID_W)
    kwin = k_ref[pl.ds(start, tk), :]
    vwin = _with_ones(v_ref[pl.ds(start, tk), :])
    for qs in range(tq // LANES):
        per_head = []
        for hh in range(2):
            rows = slice(hh * tq + qs * LANES, hh * tq + (qs + 1) * LANES)
            q = _mask_head_lanes(q_ref[qs * LANES:(qs + 1) * LANES, :], hh)
            o, _ = _softmax_pv(_nt_dot(q, kwin) + bm_ref[rows, :], vwin)
            per_head.append(o)
        o_ref[qs * LANES:(qs + 1) * LANES, :] = _pick_head_lanes(*per_head).astype(BF16)


def _attn_b(nat, bm, *, qrows):
    b, s, _ = nat.shape
    tq = qrows * GRID_W
    tk = (qrows + NA_ROWS_MAX) * GRID_W
    ngrp = s // tq
    npair = B_HEADS // 2

    def variant(g):
        return jnp.where(g > 0, 1, 0) + jnp.where(g == ngrp - 1, 1, 0)

    return pl.pallas_call(
        functools.partial(_attn_b_kernel, tq=tq, tk=tk, s_len=s),
        grid=(b, npair, ngrp),
        in_specs=[
            pl.BlockSpec((None, tq, PAIR_W), lambda bi, pr, g: (bi, g, NAT_BLK["bq"] + pr)),
            pl.BlockSpec((None, s, PAIR_W), lambda bi, pr, g: (bi, 0, NAT_BLK["bk"] + pr)),
            pl.BlockSpec((None, s, PAIR_W), lambda bi, pr, g: (bi, 0, NAT_BLK["bv"] + pr)),
            pl.BlockSpec((None, None, 2 * tq, tk), lambda bi, pr, g: (pr, variant(g), 0, 0)),
        ],
        out_specs=pl.BlockSpec((None, tq, PAIR_W), lambda bi, pr, g: (bi, g, pr)),
        out_shape=jax.ShapeDtypeStruct((b, s, B_W), BF16),
        compiler_params=_params("parallel", "parallel", "arbitrary"),
        name="attn_nbr",
    )(nat, nat, nat, bm)


def _b_bias_tiles(rpb, qrows, rows):
    kh = min(NA_ROWS_MAX, rows)
    kw = min(NA_COLS, GRID_W)
    krows = qrows + NA_ROWS_MAX
    ngrp = rows // qrows
    n_rel_r, n_rel_c = 2 * NA_ROWS_MAX - 1, 2 * NA_COLS - 1
    qc = np.arange(GRID_W)[:, None]
    kc = np.arange(GRID_W)[None, :]
    cstart = np.clip(qc - kw // 2, 0, GRID_W - kw)
    col_ok = (kc >= cstart) & (kc < cstart + kw)
    rel_c = kc - qc + (NA_COLS - 1)
    onehot_c = (rel_c[None] == np.arange(n_rel_c)[:, None, None]).astype(np.float32)
    blocks = jnp.einsum("hac,cqk->haqk", rpb.astype(F32), onehot_c, precision=lax.Precision.HIGHEST)
    blocks = jnp.where(col_ok[None, None], blocks, MASKED)
    blocks = jnp.concatenate([blocks, jnp.full((B_HEADS, 1, GRID_W, GRID_W), MASKED, F32)], axis=1)
    block_idx = []
    for g in (0, 1, ngrp - 1):
        r_abs = g * qrows + np.arange(qrows)[:, None]
        k0 = int(np.clip(g * qrows - NA_ROWS_MAX // 2, 0, rows - krows))
        kr_abs = k0 + np.arange(krows)[None, :]
        rstart = np.clip(r_abs - kh // 2, 0, rows - kh)
        row_ok = (kr_abs >= rstart) & (kr_abs < rstart + kh)
        block_idx.append(np.where(row_ok, kr_abs - r_abs + (NA_ROWS_MAX - 1), n_rel_r))
    block_idx = np.stack(block_idx)
    tiles = jnp.take(blocks, block_idx.reshape(-1), axis=1)
    tiles = tiles.reshape(B_HEADS // 2, 2, 3, qrows, krows, GRID_W, GRID_W)
    tiles = tiles.transpose(0, 2, 1, 3, 5, 4, 6)
    return tiles.reshape(B_HEADS // 2, 3, 2 * qrows * GRID_W, krows * GRID_W)


def _attn_c_kernel(tbl_ref, q_ref, k_ref, v_ref, o_ref, lse_ref, bm_ref, *, r, tq, half, s_r, unroll):
    tk = tq + 2 * half
    nblk = s_r // tq
    pr = pl.program_id(0)

    @pl.when(jnp.logical_and(pl.program_id(1) == 0, pl.program_id(2) == 0))
    def _():
        qq = lax.broadcasted_iota(jnp.int32, (tq, tk), 0)
        kk = lax.broadcasted_iota(jnp.int32, (tq, tk), 1)
        for var, off in enumerate((0, -half, -2 * half)):
            jj = kk - qq + off
            valid = jnp.abs(jj) <= half
            rel = r * jnp.clip(jj, -half, half)
            for hh in range(2):
                bias = _t5_bias(rel, tbl_ref, A_HEADS + 2 * pr + hh, r * half)
                bm_ref[var, hh * tq:(hh + 1) * tq, :] = jnp.where(valid, bias, MASKED)

    def block(i):
        q0 = pl.multiple_of(i * tq, tq)
        ws = pl.multiple_of(jnp.clip(i * tq - half, 0, s_r - tk), half)
        var = jnp.where(i > 0, 1, 0) + jnp.where(i == nblk - 1, 1, 0)
        qz = _stack_masked_q(q_ref[pl.ds(q0, tq), :])
        sc = _nt_dot(qz, k_ref[pl.ds(ws, tk), :]) + bm_ref[var]
        o, lse = _softmax_pv(sc, _with_ones(v_ref[pl.ds(ws, tk), :]))
        o_ref[pl.ds(q0, tq), :] = _pick_head_lanes(o[:tq], o[tq:]).astype(BF16)
        lse_ref[pl.ds(q0, tq), :] = _pick_head_lanes(lse[:tq], lse[tq:])

    def body(u, carry):
        for k in range(unroll):
            block(u * unroll + k)
        return carry

    lax.fori_loop(0, nblk // unroll, body, 0)


def _attn_c(c_r, t5_table, *, r, tq, half):
    b, s_r, _ = c_r.shape
    npair = C_HEADS // 2
    ncb = C_COLS // PAIR_W
    nob = C_W // PAIR_W

    def in_spec(name):
        return pl.BlockSpec((None, s_r, PAIR_W),
                            lambda pr, bi, rho: (bi, 0, rho * ncb + C_BLK[name] + pr))

    out_spec = pl.BlockSpec((None, s_r, PAIR_W), lambda pr, bi, rho: (bi, 0, rho * nob + pr))
    o, lse = pl.pallas_call(
        functools.partial(_attn_c_kernel, r=r, tq=tq, half=half, s_r=s_r,
                          unroll=math.gcd(C_UNROLL, s_r // tq)),
        grid=(npair, b, r),
        in_specs=[pl.BlockSpec(memory_space=pltpu.SMEM), in_spec("cq"), in_spec("ck"), in_spec("cv")],
        out_specs=[out_spec, out_spec],
        out_shape=[jax.ShapeDtypeStruct((b, s_r, r * C_W), BF16),
                   jax.ShapeDtypeStruct((b, s_r, r * C_W), F32)],
        scratch_shapes=[pltpu.VMEM((3, 2 * tq, tq + 2 * half), F32)],
        compiler_params=_params("arbitrary", "arbitrary", "arbitrary"),
        name=f"attn_dil{r}",
    )(t5_table, c_r, c_r, c_r)
    return o.reshape(b * s_r, r * C_W), lse.reshape(b * s_r, r * C_W)


def _merge_kernel(x_ref, g_ref, oa_ref, ob_ref, *rest, dils):
    nc = len(dils)
    oc_refs, ls_refs = rest[:nc], rest[nc:2 * nc]
    wg_ref, wa_ref, wb_ref, wc_ref, wo_ref, o_ref, tok_ref = rest[2 * nc:]
    x = x_ref[...]
    tm, d = x.shape
    h = _rms(x, g_ref[...]).astype(BF16)

    def token_order(ref, r):
        if r == 1:
            return ref[...].astype(F32)
        for rho in range(r):
            for cb in range(C_W // LANES):
                lo = rho * C_W + cb * LANES
                tok_ref[cb, pl.ds(rho, tm // r, stride=r), :] = ref[:, lo:lo + LANES].astype(F32)
        return jnp.concatenate([tok_ref[cb] for cb in range(C_W // LANES)], axis=1)

    lses = [token_order(ref, r) for ref, r in zip(ls_refs, dils)]
    mx = functools.reduce(jnp.maximum, lses)
    es = [jnp.exp(ls - mx) for ls in lses]
    num = sum(e * token_order(ref, r) for e, ref, r in zip(es, oc_refs, dils))
    oc = num / sum(es)

    merged = jnp.zeros(x.shape, F32)
    for n, (br, w_ref) in enumerate(((oa_ref[...], wa_ref), (ob_ref[...], wb_ref),
                                     (oc.astype(BF16), wc_ref))):
        gate = jnp.dot(h, wg_ref[:, n * d:(n + 1) * d], preferred_element_type=F32)
        merged = merged + jax.nn.sigmoid(gate) * jnp.dot(br, w_ref[...], preferred_element_type=F32)
    o_ref[...] = x + jnp.dot(merged.astype(BF16), wo_ref[...], preferred_element_type=F32)


def _merge(x2, g, oa, ob, ocs, lses, wg, wa, wb, wc, wo, *, tm, dils):
    n, d = x2.shape
    tok = lambda w: pl.BlockSpec((tm, w), lambda i: (i, 0))
    dil = lambda r: pl.BlockSpec((tm // r, r * C_W), lambda i: (i, 0))
    full = _resident
    return pl.pallas_call(
        functools.partial(_merge_kernel, dils=dils),
        grid=(n // tm,),
        in_specs=[tok(d), pl.BlockSpec((1, d), lambda i: (0, 0)), tok(A_W), tok(B_W)]
                 + [dil(r) for r in dils] + [dil(r) for r in dils]
                 + [full(wg), full(wa), full(wb), full(wc), full(wo)],
        out_specs=tok(d),
        out_shape=jax.ShapeDtypeStruct((n, d), F32),
        scratch_shapes=[pltpu.VMEM((C_W // LANES, tm, LANES), F32)],
        compiler_params=_params("parallel"),
        name="gated_merge",
    )(x2, g.reshape(1, d), oa, ob, *ocs, *lses, wg, wa, wb, wc, wo)


def _tiles(n_tok, s, d_ff):
    return dict(
        ffn_tm=min(1024, n_tok), ffn_tf=256 if d_ff % 256 == 0 else d_ff,
        a_t=min(512, s),
        b_qrows=4,
        c_tq=128,
        merge_tm=min(512, n_tok),
    )


def kernel(x, g_ff1, w1_ff1, w3_ff1, w2_ff1, g_mix, w_in, lam_q1, lam_k1, lam_q2, lam_k2, subln_g,
           na_rpb, t5_table, w_br_a, w_br_b, w_br_c, w_o, g_ff2, w1_ff2, w3_ff2, w2_ff2, g_final):
    b, s, d = x.shape
    depth = w_in.shape[0]
    n_tok = b * s
    rows = s // GRID_W
    tl = _tiles(n_tok, s, w1_ff1.shape[2])
    bf = lambda w: w.astype(BF16)
    c_cfgs = [(w // (2 * r), r) for (w, r) in C_CONFIGS]
    dils = tuple(r for _, r in c_cfgs)
    t5_table = t5_table.astype(F32)

    col = lambda i0, i1: slice(512 * i0, 512 * i1)

    x2 = x.reshape(n_tok, d)
    for l in range(depth):
        x2 = _ffn(x2, g_ff1[l], bf(w1_ff1[l]), bf(w3_ff1[l]), bf(w2_ff1[l]), g_final,
                  final_norm=False, tm=tl["ffn_tm"], tf=tl["ffn_tf"])

        w = w_in[l]
        w_nat = bf(jnp.concatenate([w[:, col(0, 2)], w[:, col(3, 6)]], axis=1))
        w_avt = bf(w[:, col(2, 3)].T)
        w_c = bf(w[:, col(6, 9)])
        nat, avt, *c_rs = _proj(x2.reshape(b, s, d), g_mix[l], w_nat, w_avt, w_c,
                                tm=tl["a_t"], dils=dils)

        lam_init = 0.8 - 0.6 * math.exp(-0.3 * l)
        lam_vecs = jnp.stack([lam_q1[l], lam_k1[l], lam_q2[l], lam_k2[l]])
        o_a = _attn_a(nat, avt, t5_table, lam_vecs, subln_g[l], t=tl["a_t"], lam_init=lam_init)
        o_b = _attn_b(nat, _b_bias_tiles(na_rpb[l], tl["b_qrows"], rows), qrows=tl["b_qrows"])
        c_out = [_attn_c(c_r, t5_table, r=r, tq=tl["c_tq"], half=half)
                 for c_r, (half, r) in zip(c_rs, c_cfgs)]

        x2 = _merge(x2, g_mix[l], o_a.reshape(n_tok, A_W), o_b.reshape(n_tok, B_W),
                    [o for o, _ in c_out], [ls for _, ls in c_out],
                    bf(w[:, col(9, 15)]), bf(w_br_a[l]), bf(w_br_b[l]), bf(w_br_c[l]), bf(w_o[l]),
                    tm=tl["merge_tm"], dils=dils)

        x2 = _ffn(x2, g_ff2[l], bf(w1_ff2[l]), bf(w3_ff2[l]), bf(w2_ff2[l]), g_final,
                  final_norm=(l == depth - 1), tm=tl["ffn_tm"], tf=tl["ffn_tf"])
    return x2.reshape(b, s, d)
```

```python
import functools
import math

import numpy as np
import jax
import jax.numpy as jnp
from jax import lax
from jax.experimental import pallas as pl
from jax.experimental.pallas import tpu as pltpu

HEAD_DIM = 64
A_HEADS = 4
B_HEADS = 8
C_HEADS = 8
GRID_W = 64
NA_ROWS_MAX = 8
NA_COLS = 16
C_CONFIGS = ((128, 1), (512, 4), (2048, 16))
T5_BUCKETS = 32
T5_MAX_DIST = 1024
EPS = 1e-6

LANES = 128
MXU_TILE = 256
C_UNROLL = 4
B_UNROLL = 2
VMEM_LIMIT_BYTES = 56 * 1024 * 1024

MASKED = -1e30
BF16 = jnp.bfloat16
F32 = jnp.float32

PAIR_W = 2 * HEAD_DIM
A_W = A_HEADS * PAIR_W
B_W = B_HEADS * HEAD_DIM
C_W = C_HEADS * HEAD_DIM
GROUP_W = 512
LOG2E = math.log2(math.e)
QK_SCALE = HEAD_DIM ** -0.5
NAT_BLK = {"aq": 0, "ak": 4, "bq": 8, "bk": 12, "bv": 16}
NAT_SCALE = (QK_SCALE * LOG2E, 1.0, QK_SCALE, 1.0, 1.0)
NAT_COLS = len(NAT_SCALE) * GROUP_W
C_BLK = {"cq": 0, "ck": 4, "cv": 8}
C_SCALE = (QK_SCALE, 1.0, 1.0)
C_COLS = len(C_SCALE) * GROUP_W
AVT_ROWS = PAIR_W + 16


def _params(*sem):
    return pltpu.CompilerParams(dimension_semantics=sem, vmem_limit_bytes=VMEM_LIMIT_BYTES)


def _rms(x, g):
    return x * lax.rsqrt(jnp.mean(x * x, axis=-1, keepdims=True) + EPS) * g


def _t5_abs_bucket_np(n):
    nb = T5_BUCKETS // 2
    max_exact = nb // 2
    n = np.asarray(n, np.int64)
    x = np.log(np.maximum(n, 1) / max_exact) / math.log(T5_MAX_DIST / max_exact) * (nb - max_exact)
    interior = (n > max_exact) & (n < T5_MAX_DIST)
    assert np.all(np.abs(x[interior] - np.round(x[interior])) > 1e-6)
    large = np.minimum(max_exact + np.floor(x + 1e-9).astype(np.int64), nb - 1)
    return np.where(n < max_exact, n, large)


def _t5_change_points(max_n):
    b = _t5_abs_bucket_np(np.arange(max_n + 1))
    return [(0, int(b[0]))] + [(n, int(b[n])) for n in range(1, max_n + 1) if b[n] != b[n - 1]]


def _t5_saturation(s):
    return _t5_change_points(s - 1)[-1][0]


def _t5_bias(rel, tbl_ref, head, max_n):
    nb = T5_BUCKETS // 2
    cps = _t5_change_points(max_n)
    n = jnp.abs(rel)
    neg = jnp.full(rel.shape, tbl_ref[cps[-1][1], head], F32)
    pos = jnp.full(rel.shape, tbl_ref[nb + cps[-1][1], head], F32)
    for (n0, bkt), (n1, _) in reversed(list(zip(cps[:-1], cps[1:]))):
        below = n < n1
        neg = jnp.where(below, tbl_ref[bkt, head], neg)
        pos = jnp.where(below, tbl_ref[nb + bkt, head], pos)
    return jnp.where(rel > 0, pos, neg)


def _ffn_kernel(x_ref, g_ref, w1_ref, w3_ref, w2_ref, gf_ref, o_ref, acc_ref, *, final_norm, tf):
    x = x_ref[...]
    h = _rms(x, g_ref[...]).astype(BF16)
    for c in range(w1_ref.shape[1] // tf):
        cols = slice(c * tf, (c + 1) * tf)
        a = jnp.dot(h, w1_ref[:, cols], preferred_element_type=F32)
        b = jnp.dot(h, w3_ref[:, cols], preferred_element_type=F32)
        u = (a * jax.nn.sigmoid(a) * b).astype(BF16)
        part = jnp.dot(u, w2_ref[cols, :], preferred_element_type=F32)
        if c == 0:
            acc_ref[...] = part
        else:
            acc_ref[...] += part
    y = x + 0.5 * acc_ref[...]
    if final_norm:
        y = _rms(y, gf_ref[...])
    o_ref[...] = y


def _resident(a):
    return pl.BlockSpec(a.shape, lambda *_: (0,) * a.ndim, pipeline_mode=pl.Buffered(1))


def _ffn(x2, g, w1, w3, w2, g_final, *, final_norm, tm, tf):
    n, d = x2.shape
    return pl.pallas_call(
        functools.partial(_ffn_kernel, final_norm=final_norm, tf=tf),
        grid=(n // tm,),
        in_specs=[
            pl.BlockSpec((tm, d), lambda i: (i, 0)),
            pl.BlockSpec((1, d), lambda i: (0, 0)),
            _resident(w1), _resident(w3), _resident(w2),
            pl.BlockSpec((1, d), lambda i: (0, 0)),
        ],
        out_specs=pl.BlockSpec((tm, d), lambda i: (i, 0)),
        out_shape=jax.ShapeDtypeStruct((n, d), F32),
        scratch_shapes=[pltpu.VMEM((tm, d), F32)],
        compiler_params=_params("parallel"),
        name="ffn",
    )(x2, g.reshape(1, d), w1, w3, w2, g_final.reshape(1, d))


def _proj_kernel(x_ref, g_ref, wn_ref, wvt_ref, wc_ref, nat_ref, avt_ref, *rest, dils):
    c_refs, cs_ref = rest[:-1], rest[-1]
    tm = x_ref.shape[0]
    tn = GROUP_W
    h = _rms(x_ref[...], g_ref[...]).astype(BF16)
    for c, scale in enumerate(NAT_SCALE):
        res = jnp.dot(h, wn_ref[:, c * tn:(c + 1) * tn], preferred_element_type=F32)
        nat_ref[:, c * tn:(c + 1) * tn] = (res if scale == 1.0 else res * scale).astype(BF16)
    vt = lax.dot_general(wvt_ref[...], h, (((1,), (1,)), ((), ())),
                         preferred_element_type=F32).astype(BF16)
    for hd in range(A_HEADS):
        avt_ref[hd * AVT_ROWS:hd * AVT_ROWS + PAIR_W, :] = vt[hd * PAIR_W:(hd + 1) * PAIR_W]
        avt_ref[hd * AVT_ROWS + PAIR_W:(hd + 1) * AVT_ROWS, :] = jnp.ones((AVT_ROWS - PAIR_W, tm), BF16)
    per = tn // LANES
    for c, scale in enumerate(C_SCALE):
        res = jnp.dot(h, wc_ref[:, c * tn:(c + 1) * tn], preferred_element_type=F32)
        res = res if scale == 1.0 else res * scale
        for k in range(per):
            cs_ref[c * per + k] = res[:, k * LANES:(k + 1) * LANES]
    for c_ref, r in zip(c_refs, dils):
        for rho in range(r):
            for cb in range(C_COLS // LANES):
                rows = cs_ref[cb] if r == 1 else cs_ref[cb, pl.ds(rho, tm // r, stride=r), :]
                lo = (cb * r + rho) * LANES
                c_ref[:, lo:lo + LANES] = rows.astype(BF16)


def _proj(x3, g, w_nat, w_avt, w_c, *, tm, dils):
    b, s, d = x3.shape
    nt = s // tm
    tok = lambda w: pl.BlockSpec((None, tm, w), lambda bi, i: (bi, i, 0))
    full = _resident
    out_shapes = [jax.ShapeDtypeStruct((b, s, NAT_COLS), BF16),
                  jax.ShapeDtypeStruct((b, nt, A_HEADS * AVT_ROWS, tm), BF16)]
    out_specs = [tok(NAT_COLS),
                 pl.BlockSpec((None, None, A_HEADS * AVT_ROWS, tm), lambda bi, i: (bi, i, 0, 0))]
    for r in dils:
        out_shapes.append(jax.ShapeDtypeStruct((b, s // r, r * C_COLS), BF16))
        out_specs.append(pl.BlockSpec((None, tm // r, r * C_COLS), lambda bi, i: (bi, i, 0)))
    return pl.pallas_call(
        functools.partial(_proj_kernel, dils=dils),
        grid=(b, nt),
        in_specs=[tok(d), pl.BlockSpec((1, d), lambda bi, i: (0, 0)), full(w_nat), full(w_avt), full(w_c)],
        out_specs=out_specs,
        out_shape=out_shapes,
        scratch_shapes=[pltpu.VMEM((C_COLS // LANES, tm, LANES), F32)],
        compiler_params=_params("parallel", "parallel"),
        name="qkv_proj",
    )(x3, g.reshape(1, d), w_nat, w_avt, w_c)


def _stack_masked_q(q):
    lane = lax.broadcasted_iota(jnp.int32, q.shape, 1)
    zero = jnp.zeros_like(q)
    return jnp.concatenate([jnp.where(lane < HEAD_DIM, q, zero),
                            jnp.where(lane >= HEAD_DIM, q, zero)], axis=0)


def _nt_dot(a, b):
    return lax.dot_general(a, b, (((1,), (1,)), ((), ())), preferred_element_type=F32)


def _pick_head_lanes(lo, hi):
    lane = lax.broadcasted_iota(jnp.int32, lo.shape, 1)
    return jnp.where(lane < HEAD_DIM, lo, hi)


def _mask_head_lanes(q, hh):
    lane = lax.broadcasted_iota(jnp.int32, q.shape, 1)
    keep = lane < HEAD_DIM if hh == 0 else lane >= HEAD_DIM
    return jnp.where(keep, q, jnp.zeros_like(q))


def _with_ones(v):
    return jnp.concatenate([v, jnp.ones_like(v)], axis=1)


def _softmax_pv(sc, v_ones):
    m = jnp.max(sc, axis=-1, keepdims=True)
    p = jnp.exp(sc - m).astype(BF16)
    res = jnp.dot(p, v_ones, preferred_element_type=F32)
    l = res[:, PAIR_W:]
    return res[:, :PAIR_W] / l, m + jnp.log(l)


def _attn_a_kernel(tbl_ref, lam_ref, q_ref, k_ref, vt_ref, g_ref, o_ref,
                   qz_ref, m_ref, acc_ref, bias_ref, s0_ref, s1_ref, cm0_ref, cm1_ref,
                   p0_ref, p1_ref, al0_ref, al1_ref, *, t, sw, nn, nblk, lam_init):
    h = pl.program_id(0)
    bi = pl.program_id(1)
    i = pl.program_id(2)
    nb = T5_BUCKETS // 2
    max_rel = (nn + 1) * t
    sat = _t5_change_points(max_rel)[-1][1]
    slots = ((s0_ref, cm0_ref, p0_ref, al0_ref), (s1_ref, cm1_ref, p1_ref, al1_ref))
    @pl.when(jnp.logical_and(bi == 0, i == 0))
    def _():
        kk = lax.broadcasted_iota(jnp.int32, (t, t), 0)
        qq = lax.broadcasted_iota(jnp.int32, (t, t), 1)
        bias_ref[0] = jnp.full((t, t), tbl_ref[sat, h] * LOG2E, F32)
        bias_ref[2 * nn + 2] = jnp.full((t, t), tbl_ref[nb + sat, h] * LOG2E, F32)
        for d in range(-nn, nn + 1):
            bias_ref[d + nn + 1] = _t5_bias(d * t + kk - qq, tbl_ref, h, max_rel) * LOG2E

    qz_ref[...] = _stack_masked_q(q_ref[...])
    m_ref[...] = jnp.full_like(m_ref, -jnp.inf)
    acc_ref[...] = jnp.zeros_like(acc_ref)

    n_near = 2 * nn + 1
    w0 = jnp.clip(i - nn, 0, nblk - n_near)
    c_left = tbl_ref[sat, h] * LOG2E
    c_right = tbl_ref[nb + sat, h] * LOG2E

    def block_of(pos):
        if pos < n_near:
            return w0 + pos, None
        j = jnp.where(pos - n_near < w0, pos - n_near, pos)
        return j, jnp.where(j < i, c_left, c_right)

    order = [block_of(pos) for pos in range(nblk)]

    def scores(pos, c):
        j, shift = order[pos]
        s_ref, cm_ref, _, _ = slots[pos % 2]
        cols = slice(c * sw, (c + 1) * sw)
        kb = k_ref[pl.ds(pl.multiple_of(j * t, t), t), :]
        s = _nt_dot(kb, qz_ref[cols, :])
        if shift is None:
            tcols = slice((c * sw) % t, (c * sw) % t + sw)
            s = s + bias_ref[jnp.clip(j - i, -(nn + 1), nn + 1) + nn + 1, :, tcols]
        s_ref[:, cols] = s
        cmax = jnp.max(s, axis=0, keepdims=True)
        cm_ref[:, cols] = cmax if shift is None else cmax + shift

    def softmax(pos, c):
        _, shift = order[pos]
        s_ref, cm_ref, p_ref, al_ref = slots[pos % 2]
        cols = slice(c * sw, (c + 1) * sw)
        m_prev = m_ref[:, cols]
        m_new = jnp.maximum(m_prev, cm_ref[:, cols])
        al_ref[:, cols] = jnp.exp2(m_prev - m_new)
        sub = m_new if shift is None else m_new - shift
        p_ref[:, cols] = jnp.exp2(s_ref[:, cols] - sub).astype(BF16)
        m_ref[:, cols] = m_new

    def values(pos, c):
        j, _ = order[pos]
        _, _, p_ref, al_ref = slots[pos % 2]
        cols = slice(c * sw, (c + 1) * sw)
        acc_ref[:, cols] = al_ref[:, cols] * acc_ref[:, cols] + jnp.dot(
            vt_ref[j], p_ref[:, cols], preferred_element_type=F32)

    for tau in range(nblk + 2):
        for c in range(2 * t // sw):
            if tau < nblk:
                scores(tau, c)
            if 0 <= tau - 1 < nblk:
                softmax(tau - 1, c)
            if 0 <= tau - 2 < nblk:
                values(tau - 2, c)

    lam = (jnp.exp(jnp.sum(lam_ref[0:1, :] * lam_ref[1:2, :], axis=-1, keepdims=True))
           - jnp.exp(jnp.sum(lam_ref[2:3, :] * lam_ref[3:4, :], axis=-1, keepdims=True)) + lam_init)
    o = acc_ref[0:PAIR_W, :] / acc_ref[PAIR_W:PAIR_W + 1, :]
    o = o[:, :t] - lam * o[:, t:]
    o = o * lax.rsqrt(jnp.mean(o * o, axis=0, keepdims=True) + EPS) * g_ref[...] * (1.0 - lam_init)
    o_ref[...] = o.T.astype(BF16)


def _attn_a(nat, avt, t5_table, lam, subln_g, *, t, lam_init):
    b, s, _ = nat.shape
    nblk = s // t
    sat_n = _t5_saturation(s)
    nn = -(-(sat_n - 1) // t)
    assert nblk >= 2 * nn + 1, "the near-diagonal window must fit in the sequence"
    slot_scratch = [pltpu.VMEM((t, 2 * t), F32)] * 2 + [pltpu.VMEM((1, 2 * t), F32)] * 2 \
        + [pltpu.VMEM((t, 2 * t), BF16)] * 2 + [pltpu.VMEM((1, 2 * t), F32)] * 2
    return pl.pallas_call(
        functools.partial(_attn_a_kernel, t=t, sw=min(MXU_TILE, t), nn=nn, nblk=nblk,
                          lam_init=lam_init),
        grid=(A_HEADS, b, nblk),
        in_specs=[
            pl.BlockSpec(memory_space=pltpu.SMEM),
            pl.BlockSpec((4, HEAD_DIM), lambda h, bi, i: (0, 0)),
            pl.BlockSpec((None, t, PAIR_W), lambda h, bi, i: (bi, i, NAT_BLK["aq"] + h)),
            pl.BlockSpec((None, s, PAIR_W), lambda h, bi, i: (bi, 0, NAT_BLK["ak"] + h)),
            pl.BlockSpec((None, nblk, AVT_ROWS, t), lambda h, bi, i: (bi, 0, h, 0)),
            pl.BlockSpec((PAIR_W, 1), lambda h, bi, i: (0, 0)),
        ],
        out_specs=pl.BlockSpec((None, t, PAIR_W), lambda h, bi, i: (bi, i, h)),
        out_shape=jax.ShapeDtypeStruct((b, s, A_W), BF16),
        scratch_shapes=[
            pltpu.VMEM((2 * t, PAIR_W), BF16),
            pltpu.VMEM((1, 2 * t), F32),
            pltpu.VMEM((AVT_ROWS, 2 * t), F32),
            pltpu.VMEM((2 * nn + 3, t, t), F32),
        ] + slot_scratch,
        compiler_params=_params("arbitrary", "arbitrary", "arbitrary"),
        name="attn_diff",
    )(t5_table, lam, nat, nat, avt, subln_g.reshape(PAIR_W, 1))


def _attn_b_kernel(q_ref, k_ref, v_ref, bm_ref, o_ref, *, tq, tk, s_len, unroll):
    ngrp = s_len // tq

    def group(g):
        q0 = pl.multiple_of(g * tq, tq)
        start = pl.multiple_of(jnp.clip(g * tq - (NA_ROWS_MAX // 2) * GRID_W, 0, s_len - tk), GRID_W)
        var = jnp.where(g > 0, 1, 0) + jnp.where(g == ngrp - 1, 1, 0)
        kwin = k_ref[pl.ds(start, tk), :]
        vwin = _with_ones(v_ref[pl.ds(start, tk), :])
        for qs in range(tq // LANES):
            per_head = []
            for hh in range(2):
                rows = slice(hh * tq + qs * LANES, hh * tq + (qs + 1) * LANES)
                q = _mask_head_lanes(q_ref[pl.ds(q0 + qs * LANES, LANES), :], hh)
                o, _ = _softmax_pv(_nt_dot(q, kwin) + bm_ref[var, rows, :], vwin)
                per_head.append(o)
            o_ref[pl.ds(q0 + qs * LANES, LANES), :] = _pick_head_lanes(*per_head).astype(BF16)

    def body(u, carry):
        for k in range(unroll):
            group(u * unroll + k)
        return carry

    lax.fori_loop(0, ngrp // unroll, body, 0)


def _attn_b(nat, bm, *, qrows):
    b, s, _ = nat.shape
    tq = qrows * GRID_W
    tk = (qrows + NA_ROWS_MAX) * GRID_W
    ngrp = s // tq
    npair = B_HEADS // 2

    def seq(name):
        return pl.BlockSpec((None, s, PAIR_W), lambda pr, bi: (bi, 0, NAT_BLK[name] + pr))

    return pl.pallas_call(
        functools.partial(_attn_b_kernel, tq=tq, tk=tk, s_len=s, unroll=math.gcd(B_UNROLL, ngrp)),
        grid=(npair, b),
        in_specs=[seq("bq"), seq("bk"), seq("bv"),
                  pl.BlockSpec((None, 3, 2 * tq, tk), lambda pr, bi: (pr, 0, 0, 0))],
        out_specs=pl.BlockSpec((None, s, PAIR_W), lambda pr, bi: (bi, 0, pr)),
        out_shape=jax.ShapeDtypeStruct((b, s, B_W), BF16),
        compiler_params=_params("parallel", "parallel"),
        name="attn_nbr",
    )(nat, nat, nat, bm)


def _b_bias_tiles(rpb, qrows, rows):
    kh = min(NA_ROWS_MAX, rows)
    kw = min(NA_COLS, GRID_W)
    krows = qrows + NA_ROWS_MAX
    ngrp = rows // qrows
    n_rel_r, n_rel_c = 2 * NA_ROWS_MAX - 1, 2 * NA_COLS - 1
    qc = np.arange(GRID_W)[:, None]
    kc = np.arange(GRID_W)[None, :]
    cstart = np.clip(qc - kw // 2, 0, GRID_W - kw)
    col_ok = (kc >= cstart) & (kc < cstart + kw)
    rel_c = kc - qc + (NA_COLS - 1)
    onehot_c = (rel_c[None] == np.arange(n_rel_c)[:, None, None]).astype(np.float32)
    blocks = jnp.einsum("hac,cqk->haqk", rpb.astype(F32), onehot_c, precision=lax.Precision.HIGHEST)
    blocks = jnp.where(col_ok[None, None], blocks, MASKED)
    blocks = jnp.concatenate([blocks, jnp.full((B_HEADS, 1, GRID_W, GRID_W), MASKED, F32)], axis=1)
    block_idx = []
    for g in (0, 1, ngrp - 1):
        r_abs = g * qrows + np.arange(qrows)[:, None]
        k0 = int(np.clip(g * qrows - NA_ROWS_MAX // 2, 0, rows - krows))
        kr_abs = k0 + np.arange(krows)[None, :]
        rstart = np.clip(r_abs - kh // 2, 0, rows - kh)
        row_ok = (kr_abs >= rstart) & (kr_abs < rstart + kh)
        block_idx.append(np.where(row_ok, kr_abs - r_abs + (NA_ROWS_MAX - 1), n_rel_r))
    block_idx = np.stack(block_idx)
    tiles = jnp.take(blocks, block_idx.reshape(-1), axis=1)
    tiles = tiles.reshape(B_HEADS // 2, 2, 3, qrows, krows, GRID_W, GRID_W)
    tiles = tiles.transpose(0, 2, 1, 3, 5, 4, 6)
    return tiles.reshape(B_HEADS // 2, 3, 2 * qrows * GRID_W, krows * GRID_W)


def _attn_c_kernel(tbl_ref, q_ref, k_ref, v_ref, o_ref, lse_ref, bm_ref, *, r, tq, half, s_r, unroll):
    tk = tq + 2 * half
    nblk = s_r // tq
    pr = pl.program_id(0)

    @pl.when(pl.program_id(1) == 0)
    def _():
        qq = lax.broadcasted_iota(jnp.int32, (tq, tk), 0)
        kk = lax.broadcasted_iota(jnp.int32, (tq, tk), 1)
        for var, off in enumerate((0, -half, -2 * half)):
            jj = kk - qq + off
            valid = jnp.abs(jj) <= half
            rel = r * jnp.clip(jj, -half, half)
            for hh in range(2):
                bias = _t5_bias(rel, tbl_ref, A_HEADS + 2 * pr + hh, r * half)
                bm_ref[var, hh * tq:(hh + 1) * tq, :] = jnp.where(valid, bias, MASKED)

    def block(i, lanes):
        q0 = pl.multiple_of(i * tq, tq)
        ws = pl.multiple_of(jnp.clip(i * tq - half, 0, s_r - tk), half)
        var = jnp.where(i > 0, 1, 0) + jnp.where(i == nblk - 1, 1, 0)
        qz = _stack_masked_q(q_ref[pl.ds(q0, tq), lanes])
        sc = _nt_dot(qz, k_ref[pl.ds(ws, tk), lanes]) + bm_ref[var]
        o, lse = _softmax_pv(sc, _with_ones(v_ref[pl.ds(ws, tk), lanes]))
        o_ref[pl.ds(q0, tq), lanes] = _pick_head_lanes(o[:tq], o[tq:]).astype(BF16)
        lse_ref[pl.ds(q0, tq), lanes] = _pick_head_lanes(lse[:tq], lse[tq:])

    for rho in range(r):
        lanes = slice(rho * LANES, (rho + 1) * LANES)

        def body(u, carry, lanes=lanes):
            for k in range(unroll):
                block(u * unroll + k, lanes)
            return carry

        if nblk == unroll:
            body(0, 0)
        else:
            lax.fori_loop(0, nblk // unroll, body, 0)


def _attn_c(c_r, t5_table, *, r, tq, half):
    b, s_r, _ = c_r.shape
    npair = C_HEADS // 2

    def in_spec(name):
        return pl.BlockSpec((None, s_r, r * PAIR_W), lambda pr, bi: (bi, 0, C_BLK[name] + pr))

    out_spec = pl.BlockSpec((None, s_r, r * PAIR_W), lambda pr, bi: (bi, 0, pr))
    o, lse = pl.pallas_call(
        functools.partial(_attn_c_kernel, r=r, tq=tq, half=half, s_r=s_r,
                          unroll=math.gcd(C_UNROLL, s_r // tq)),
        grid=(npair, b),
        in_specs=[pl.BlockSpec(memory_space=pltpu.SMEM), in_spec("cq"), in_spec("ck"), in_spec("cv")],
        out_specs=[out_spec, out_spec],
        out_shape=[jax.ShapeDtypeStruct((b, s_r, r * C_W), BF16),
                   jax.ShapeDtypeStruct((b, s_r, r * C_W), F32)],
        scratch_shapes=[pltpu.VMEM((3, 2 * tq, tq + 2 * half), F32)],
        compiler_params=_params("arbitrary", "arbitrary"),
        name=f"attn_dil{r}",
    )(t5_table, c_r, c_r, c_r)
    return o.reshape(b * s_r, r * C_W), lse.reshape(b * s_r, r * C_W)


def _merge_kernel(x_ref, g_ref, oa_ref, ob_ref, *rest, dils):
    nc = len(dils)
    oc_refs, ls_refs = rest[:nc], rest[nc:2 * nc]
    wg_ref, wa_ref, wb_ref, wc_ref, wo_ref, o_ref, tok_ref = rest[2 * nc:]
    x = x_ref[...]
    tm, d = x.shape
    h = _rms(x, g_ref[...]).astype(BF16)

    def token_order(ref, r):
        if r == 1:
            return ref[...].astype(F32)
        for rho in range(r):
            for cb in range(C_W // LANES):
                lo = (cb * r + rho) * LANES
                tok_ref[cb, pl.ds(rho, tm // r, stride=r), :] = ref[:, lo:lo + LANES].astype(F32)
        return jnp.concatenate([tok_ref[cb] for cb in range(C_W // LANES)], axis=1)

    lses = [token_order(ref, r) for ref, r in zip(ls_refs, dils)]
    mx = functools.reduce(jnp.maximum, lses)
    es = [jnp.exp(ls - mx) for ls in lses]
    num = sum(e * token_order(ref, r) for e, ref, r in zip(es, oc_refs, dils))
    oc = num / sum(es)

    merged = jnp.zeros(x.shape, F32)
    for n, (br, w_ref) in enumerate(((oa_ref[...], wa_ref), (ob_ref[...], wb_ref),
                                     (oc.astype(BF16), wc_ref))):
        gate = jnp.dot(h, wg_ref[:, n * d:(n + 1) * d], preferred_element_type=F32)
        merged = merged + jax.nn.sigmoid(gate) * jnp.dot(br, w_ref[...], preferred_element_type=F32)
    o_ref[...] = x + jnp.dot(merged.astype(BF16), wo_ref[...], preferred_element_type=F32)


def _merge(x2, g, oa, ob, ocs, lses, wg, wa, wb, wc, wo, *, tm, dils):
    n, d = x2.shape
    tok = lambda w: pl.BlockSpec((tm, w), lambda i: (i, 0))
    dil = lambda r: pl.BlockSpec((tm // r, r * C_W), lambda i: (i, 0))
    full = _resident
    return pl.pallas_call(
        functools.partial(_merge_kernel, dils=dils),
        grid=(n // tm,),
        in_specs=[tok(d), pl.BlockSpec((1, d), lambda i: (0, 0)), tok(A_W), tok(B_W)]
                 + [dil(r) for r in dils] + [dil(r) for r in dils]
                 + [full(wg), full(wa), full(wb), full(wc), full(wo)],
        out_specs=tok(d),
        out_shape=jax.ShapeDtypeStruct((n, d), F32),
        scratch_shapes=[pltpu.VMEM((C_W // LANES, tm, LANES), F32)],
        compiler_params=_params("parallel"),
        name="gated_merge",
    )(x2, g.reshape(1, d), oa, ob, *ocs, *lses, wg, wa, wb, wc, wo)


def _tiles(n_tok, s, d_ff):
    return dict(
        ffn_tm=min(1024, n_tok), ffn_tf=256 if d_ff % 256 == 0 else d_ff,
        a_t=min(512, s),
        b_qrows=4,
        c_tq=128,
        merge_tm=min(512, n_tok),
    )


def kernel(x, g_ff1, w1_ff1, w3_ff1, w2_ff1, g_mix, w_in, lam_q1, lam_k1, lam_q2, lam_k2, subln_g,
           na_rpb, t5_table, w_br_a, w_br_b, w_br_c, w_o, g_ff2, w1_ff2, w3_ff2, w2_ff2, g_final):
    b, s, d = x.shape
    depth = w_in.shape[0]
    n_tok = b * s
    rows = s // GRID_W
    tl = _tiles(n_tok, s, w1_ff1.shape[2])
    bf = lambda w: w.astype(BF16)
    c_cfgs = [(w // (2 * r), r) for (w, r) in C_CONFIGS]
    dils = tuple(r for _, r in c_cfgs)
    t5_table = t5_table.astype(F32)

    col = lambda i0, i1: slice(512 * i0, 512 * i1)

    x2 = x.reshape(n_tok, d)
    for l in range(depth):
        x2 = _ffn(x2, g_ff1[l], bf(w1_ff1[l]), bf(w3_ff1[l]), bf(w2_ff1[l]), g_final,
                  final_norm=False, tm=tl["ffn_tm"], tf=tl["ffn_tf"])

        w = w_in[l]
        w_nat = bf(jnp.concatenate([w[:, col(0, 2)], w[:, col(3, 6)]], axis=1))
        w_avt = bf(w[:, col(2, 3)].T)
        w_c = bf(w[:, col(6, 9)])
        nat, avt, *c_rs = _proj(x2.reshape(b, s, d), g_mix[l], w_nat, w_avt, w_c,
                                tm=tl["a_t"], dils=dils)

        lam_init = 0.8 - 0.6 * math.exp(-0.3 * l)
        lam_vecs = jnp.stack([lam_q1[l], lam_k1[l], lam_q2[l], lam_k2[l]])
        o_a = _attn_a(nat, avt, t5_table, lam_vecs, subln_g[l], t=tl["a_t"], lam_init=lam_init)
        o_b = _attn_b(nat, _b_bias_tiles(na_rpb[l], tl["b_qrows"], rows), qrows=tl["b_qrows"])
        c_out = [_attn_c(c_r, t5_table, r=r, tq=tl["c_tq"], half=half)
                 for c_r, (half, r) in zip(c_rs, c_cfgs)]

        x2 = _merge(x2, g_mix[l], o_a.reshape(n_tok, A_W), o_b.reshape(n_tok, B_W),
                    [o for o, _ in c_out], [ls for _, ls in c_out],
                    bf(w[:, col(9, 15)]), bf(w_br_a[l]), bf(w_br_b[l]), bf(w_br_c[l]), bf(w_o[l]),
                    tm=tl["merge_tm"], dils=dils)

        x2 = _ffn(x2, g_ff2[l], bf(w1_ff2[l]), bf(w3_ff2[l]), bf(w2_ff2[l]), g_final,
                  final_norm=(l == depth - 1), tm=tl["ffn_tm"], tf=tl["ffn_tf"])
    return x2.reshape(b, s, d)
```

```python
import functools
import math

import numpy as np
import jax
import jax.numpy as jnp
from jax import lax
from jax.experimental import pallas as pl
from jax.experimental.pallas import tpu as pltpu

HEAD_DIM = 64
A_HEADS = 4
B_HEADS = 8
C_HEADS = 8
GRID_W = 64
NA_ROWS_MAX = 8
NA_COLS = 16
C_CONFIGS = ((128, 1), (512, 4), (2048, 16))
T5_BUCKETS = 32
T5_MAX_DIST = 1024
EPS = 1e-6

LANES = 128
MXU_TILE = 256
C_UNROLL = 16
B_UNROLL = 4
VMEM_LIMIT_BYTES = 56 * 1024 * 1024

MASKED = -1e30
BF16 = jnp.bfloat16
F32 = jnp.float32

PAIR_W = 2 * HEAD_DIM
A_W = A_HEADS * PAIR_W
B_W = B_HEADS * HEAD_DIM
C_W = C_HEADS * HEAD_DIM
GROUP_W = 512
LOG2E = math.log2(math.e)
QK_SCALE = HEAD_DIM ** -0.5
NAT_BLK = {"aq": 0, "ak": 4, "bq": 8, "bk": 12, "bv": 16}
NAT_SCALE = (QK_SCALE * LOG2E, 1.0, QK_SCALE, 1.0, 1.0)
NAT_COLS = len(NAT_SCALE) * GROUP_W
C_BLK = {"cq": 0, "ck": 4, "cv": 8}
C_SCALE = (QK_SCALE, 1.0, 1.0)
C_COLS = len(C_SCALE) * GROUP_W
AVT_ROWS = PAIR_W + 16


def _params(*sem):
    return pltpu.CompilerParams(dimension_semantics=sem, vmem_limit_bytes=VMEM_LIMIT_BYTES)


def _rms(x, g):
    return x * lax.rsqrt(jnp.mean(x * x, axis=-1, keepdims=True) + EPS) * g


def _t5_abs_bucket_np(n):
    nb = T5_BUCKETS // 2
    max_exact = nb // 2
    n = np.asarray(n, np.int64)
    x = np.log(np.maximum(n, 1) / max_exact) / math.log(T5_MAX_DIST / max_exact) * (nb - max_exact)
    interior = (n > max_exact) & (n < T5_MAX_DIST)
    assert np.all(np.abs(x[interior] - np.round(x[interior])) > 1e-6)
    large = np.minimum(max_exact + np.floor(x + 1e-9).astype(np.int64), nb - 1)
    return np.where(n < max_exact, n, large)


def _t5_change_points(max_n):
    b = _t5_abs_bucket_np(np.arange(max_n + 1))
    return [(0, int(b[0]))] + [(n, int(b[n])) for n in range(1, max_n + 1) if b[n] != b[n - 1]]


def _t5_saturation(s):
    return _t5_change_points(s - 1)[-1][0]


def _t5_bias(rel, tbl_ref, head, max_n):
    nb = T5_BUCKETS // 2
    cps = _t5_change_points(max_n)
    n = jnp.abs(rel)
    neg = jnp.full(rel.shape, tbl_ref[cps[-1][1], head], F32)
    pos = jnp.full(rel.shape, tbl_ref[nb + cps[-1][1], head], F32)
    for (n0, bkt), (n1, _) in reversed(list(zip(cps[:-1], cps[1:]))):
        below = n < n1
        neg = jnp.where(below, tbl_ref[bkt, head], neg)
        pos = jnp.where(below, tbl_ref[nb + bkt, head], pos)
    return jnp.where(rel > 0, pos, neg)


def _ffn_kernel(x_ref, g_ref, w1_ref, w3_ref, w2_ref, gf_ref, o_ref, acc_ref, *, final_norm, tf):
    x = x_ref[...]
    h = _rms(x, g_ref[...]).astype(BF16)
    for c in range(w1_ref.shape[1] // tf):
        cols = slice(c * tf, (c + 1) * tf)
        a = jnp.dot(h, w1_ref[:, cols], preferred_element_type=F32)
        b = jnp.dot(h, w3_ref[:, cols], preferred_element_type=F32)
        u = (a * jax.nn.sigmoid(a) * b).astype(BF16)
        part = jnp.dot(u, w2_ref[cols, :], preferred_element_type=F32)
        if c == 0:
            acc_ref[...] = part
        else:
            acc_ref[...] += part
    y = x + 0.5 * acc_ref[...]
    if final_norm:
        y = _rms(y, gf_ref[...])
    o_ref[...] = y


def _resident(a):
    return pl.BlockSpec(a.shape, lambda *_: (0,) * a.ndim, pipeline_mode=pl.Buffered(1))


def _ffn(x2, g, w1, w3, w2, g_final, *, final_norm, tm, tf):
    n, d = x2.shape
    return pl.pallas_call(
        functools.partial(_ffn_kernel, final_norm=final_norm, tf=tf),
        grid=(n // tm,),
        in_specs=[
            pl.BlockSpec((tm, d), lambda i: (i, 0)),
            pl.BlockSpec((1, d), lambda i: (0, 0)),
            _resident(w1), _resident(w3), _resident(w2),
            pl.BlockSpec((1, d), lambda i: (0, 0)),
        ],
        out_specs=pl.BlockSpec((tm, d), lambda i: (i, 0)),
        out_shape=jax.ShapeDtypeStruct((n, d), F32),
        scratch_shapes=[pltpu.VMEM((tm, d), F32)],
        compiler_params=_params("parallel"),
        name="ffn",
    )(x2, g.reshape(1, d), w1, w3, w2, g_final.reshape(1, d))


def _proj_kernel(x_ref, g_ref, wn_ref, wvt_ref, wc_ref, nat_ref, avt_ref, *rest, dils):
    c_refs, cs_ref = rest[:-1], rest[-1]
    tm = x_ref.shape[0]
    tn = GROUP_W
    h = _rms(x_ref[...], g_ref[...]).astype(BF16)
    for c, scale in enumerate(NAT_SCALE):
        res = jnp.dot(h, wn_ref[:, c * tn:(c + 1) * tn], preferred_element_type=F32)
        nat_ref[:, c * tn:(c + 1) * tn] = (res if scale == 1.0 else res * scale).astype(BF16)
    vt = lax.dot_general(wvt_ref[...], h, (((1,), (1,)), ((), ())),
                         preferred_element_type=F32).astype(BF16)
    for hd in range(A_HEADS):
        for kb in range(avt_ref.shape[0]):
            tk = avt_ref.shape[2]
            avt_ref[kb, hd * AVT_ROWS:hd * AVT_ROWS + PAIR_W, :] = (
                vt[hd * PAIR_W:(hd + 1) * PAIR_W, kb * tk:(kb + 1) * tk])
            avt_ref[kb, hd * AVT_ROWS + PAIR_W:(hd + 1) * AVT_ROWS, :] = (
                jnp.ones((AVT_ROWS - PAIR_W, tk), BF16))
    per = tn // LANES
    for c, scale in enumerate(C_SCALE):
        res = jnp.dot(h, wc_ref[:, c * tn:(c + 1) * tn], preferred_element_type=F32)
        res = res if scale == 1.0 else res * scale
        for k in range(per):
            cs_ref[c * per + k] = res[:, k * LANES:(k + 1) * LANES]
    for c_ref, r in zip(c_refs, dils):
        for rho in range(r):
            for cb in range(C_COLS // LANES):
                rows = cs_ref[cb] if r == 1 else cs_ref[cb, pl.ds(rho, tm // r, stride=r), :]
                lo = (cb * r + rho) * LANES
                c_ref[:, lo:lo + LANES] = rows.astype(BF16)


def _proj(x3, g, w_nat, w_avt, w_c, *, tm, tk, dils):
    b, s, d = x3.shape
    nt = s // tm
    tok = lambda w: pl.BlockSpec((None, tm, w), lambda bi, i: (bi, i, 0))
    full = _resident
    out_shapes = [jax.ShapeDtypeStruct((b, s, NAT_COLS), BF16),
                  jax.ShapeDtypeStruct((b, s // tk, A_HEADS * AVT_ROWS, tk), BF16)]
    out_specs = [tok(NAT_COLS),
                 pl.BlockSpec((None, tm // tk, A_HEADS * AVT_ROWS, tk), lambda bi, i: (bi, i, 0, 0))]
    for r in dils:
        out_shapes.append(jax.ShapeDtypeStruct((b, s // r, r * C_COLS), BF16))
        out_specs.append(pl.BlockSpec((None, tm // r, r * C_COLS), lambda bi, i: (bi, i, 0)))
    return pl.pallas_call(
        functools.partial(_proj_kernel, dils=dils),
        grid=(b, nt),
        in_specs=[tok(d), pl.BlockSpec((1, d), lambda bi, i: (0, 0)), full(w_nat), full(w_avt), full(w_c)],
        out_specs=out_specs,
        out_shape=out_shapes,
        scratch_shapes=[pltpu.VMEM((C_COLS // LANES, tm, LANES), F32)],
        compiler_params=_params("parallel", "parallel"),
        name="qkv_proj",
    )(x3, g.reshape(1, d), w_nat, w_avt, w_c)


def _stack_masked_q(q):
    lane = lax.broadcasted_iota(jnp.int32, q.shape, 1)
    zero = jnp.zeros_like(q)
    return jnp.concatenate([jnp.where(lane < HEAD_DIM, q, zero),
                            jnp.where(lane >= HEAD_DIM, q, zero)], axis=0)


def _nt_dot(a, b):
    return lax.dot_general(a, b, (((1,), (1,)), ((), ())), preferred_element_type=F32)


def _pick_head_lanes(lo, hi):
    lane = lax.broadcasted_iota(jnp.int32, lo.shape, 1)
    return jnp.where(lane < HEAD_DIM, lo, hi)


def _mask_head_lanes(q, hh):
    lane = lax.broadcasted_iota(jnp.int32, q.shape, 1)
    keep = lane < HEAD_DIM if hh == 0 else lane >= HEAD_DIM
    return jnp.where(keep, q, jnp.zeros_like(q))


def _with_ones(v):
    return jnp.concatenate([v, jnp.ones_like(v)], axis=1)


def _softmax_pv(sc, v_ones):
    m = jnp.max(sc, axis=-1, keepdims=True)
    p = jnp.exp(sc - m).astype(BF16)
    res = jnp.dot(p, v_ones, preferred_element_type=F32)
    l = res[:, PAIR_W:]
    return res[:, :PAIR_W] / l, m + jnp.log(l)


def _attn_a_kernel(tbl_ref, lam_ref, q_ref, k_ref, vt_ref, g_ref, o_ref,
                   qz_ref, m_ref, acc_ref, bias_ref, s0_ref, s1_ref, cm0_ref, cm1_ref,
                   p0_ref, p1_ref, al0_ref, al1_ref, *, tq, tk, sw, d_lo, d_hi, nblk, lam_init):
    h = pl.program_id(0)
    bi = pl.program_id(1)
    i = pl.program_id(2)
    nb = T5_BUCKETS // 2
    max_rel = max(tq - 1 - d_lo * tk, d_hi * tk + tk - 1)
    sat = _t5_change_points(max_rel)[-1][1]
    n_near = d_hi - d_lo + 1
    j0 = i * (tq // tk)

    @pl.when(jnp.logical_and(bi == 0, i == 0))
    def _():
        kk = lax.broadcasted_iota(jnp.int32, (tk, tq), 0)
        qq = lax.broadcasted_iota(jnp.int32, (tk, tq), 1)
        bias_ref[0] = jnp.full((tk, tq), tbl_ref[sat, h] * LOG2E, F32)
        bias_ref[n_near + 1] = jnp.full((tk, tq), tbl_ref[nb + sat, h] * LOG2E, F32)
        for d in range(d_lo, d_hi + 1):
            bias_ref[d - d_lo + 1] = _t5_bias(d * tk + kk - qq, tbl_ref, h, max_rel) * LOG2E

    qz_ref[...] = _stack_masked_q(q_ref[...])
    m_ref[...] = jnp.full_like(m_ref, -jnp.inf)
    acc_ref[...] = jnp.zeros_like(acc_ref)

    w0 = jnp.clip(j0 + d_lo, 0, nblk - n_near)
    c_left = tbl_ref[sat, h] * LOG2E
    c_right = tbl_ref[nb + sat, h] * LOG2E

    def block_of(pos):
        if pos < n_near:
            return w0 + pos, None
        j = jnp.where(pos - n_near < w0, pos - n_near, pos)
        return j, jnp.where(j < j0, c_left, c_right)

    order = [block_of(pos) for pos in range(nblk)]

    slots = ((s0_ref, cm0_ref, p0_ref, al0_ref), (s1_ref, cm1_ref, p1_ref, al1_ref))

    def scores(pos, c):
        j, shift = order[pos]
        s_ref, cm_ref, _, _ = slots[pos % 2]
        cols = slice(c * sw, (c + 1) * sw)
        kb = k_ref[pl.ds(pl.multiple_of(j * tk, tk), tk), :]
        s = _nt_dot(kb, qz_ref[cols, :])
        if shift is None:
            tcols = slice((c * sw) % tq, (c * sw) % tq + sw)
            s = s + bias_ref[jnp.clip(j - j0, d_lo - 1, d_hi + 1) - (d_lo - 1), :, tcols]
        s_ref[:, cols] = s
        cmax = jnp.max(s, axis=0, keepdims=True)
        cm_ref[:, cols] = cmax if shift is None else cmax + shift

    def softmax(pos, c):
        _, shift = order[pos]
        s_ref, cm_ref, p_ref, al_ref = slots[pos % 2]
        cols = slice(c * sw, (c + 1) * sw)
        m_prev = m_ref[:, cols]
        m_new = jnp.maximum(m_prev, cm_ref[:, cols])
        al_ref[:, cols] = jnp.exp2(m_prev - m_new)
        sub = m_new if shift is None else m_new - shift
        p_ref[:, cols] = jnp.exp2(s_ref[:, cols] - sub).astype(BF16)
        m_ref[:, cols] = m_new

    def values(pos, c):
        j, _ = order[pos]
        _, _, p_ref, al_ref = slots[pos % 2]
        cols = slice(c * sw, (c + 1) * sw)
        acc_ref[:, cols] = al_ref[:, cols] * acc_ref[:, cols] + jnp.dot(
            vt_ref[j], p_ref[:, cols], preferred_element_type=F32)

    for tau in range(nblk + 2):
        for c in range(2 * tq // sw):
            if tau < nblk:
                scores(tau, c)
            if 0 <= tau - 1 < nblk:
                softmax(tau - 1, c)
            if 0 <= tau - 2 < nblk:
                values(tau - 2, c)

    lam = (jnp.exp(jnp.sum(lam_ref[0:1, :] * lam_ref[1:2, :], axis=-1, keepdims=True))
           - jnp.exp(jnp.sum(lam_ref[2:3, :] * lam_ref[3:4, :], axis=-1, keepdims=True)) + lam_init)
    o = acc_ref[0:PAIR_W, :] / acc_ref[PAIR_W:PAIR_W + 1, :]
    o = o[:, :tq] - lam * o[:, tq:]
    o = o * lax.rsqrt(jnp.mean(o * o, axis=0, keepdims=True) + EPS) * g_ref[...] * (1.0 - lam_init)
    o_ref[...] = o.T.astype(BF16)


def _attn_a(nat, avt, t5_table, lam, subln_g, *, tq, tk, lam_init):
    b, s, _ = nat.shape
    nblk = s // tk
    sat_n = _t5_saturation(s)
    d_lo = (-sat_n - tk + 1) // tk + 1
    d_hi = -(-(sat_n + tq - 1) // tk) - 1
    assert nblk >= d_hi - d_lo + 1, "the near-diagonal window must fit in the sequence"
    return pl.pallas_call(
        functools.partial(_attn_a_kernel, tq=tq, tk=tk, sw=min(MXU_TILE, tq), d_lo=d_lo, d_hi=d_hi,
                          nblk=nblk, lam_init=lam_init),
        grid=(A_HEADS, b, s // tq),
        in_specs=[
            pl.BlockSpec(memory_space=pltpu.SMEM),
            pl.BlockSpec((4, HEAD_DIM), lambda h, bi, i: (0, 0)),
            pl.BlockSpec((None, tq, PAIR_W), lambda h, bi, i: (bi, i, NAT_BLK["aq"] + h)),
            pl.BlockSpec((None, s, PAIR_W), lambda h, bi, i: (bi, 0, NAT_BLK["ak"] + h)),
            pl.BlockSpec((None, nblk, AVT_ROWS, tk), lambda h, bi, i: (bi, 0, h, 0)),
            pl.BlockSpec((PAIR_W, 1), lambda h, bi, i: (0, 0)),
        ],
        out_specs=pl.BlockSpec((None, tq, PAIR_W), lambda h, bi, i: (bi, i, h)),
        out_shape=jax.ShapeDtypeStruct((b, s, A_W), BF16),
        scratch_shapes=[
            pltpu.VMEM((2 * tq, PAIR_W), BF16),
            pltpu.VMEM((1, 2 * tq), F32),
            pltpu.VMEM((AVT_ROWS, 2 * tq), F32),
            pltpu.VMEM((d_hi - d_lo + 3, tk, tq), F32),
        ] + [pltpu.VMEM((tk, 2 * tq), F32)] * 2 + [pltpu.VMEM((1, 2 * tq), F32)] * 2
          + [pltpu.VMEM((tk, 2 * tq), BF16)] * 2 + [pltpu.VMEM((1, 2 * tq), F32)] * 2,
        compiler_params=_params("arbitrary", "arbitrary", "arbitrary"),
        name="attn_diff",
    )(t5_table, lam, nat, nat, avt, subln_g.reshape(PAIR_W, 1))


def _attn_b_kernel(q_ref, k_ref, v_ref, bm_ref, o_ref, *, tq, tk, s_len, unroll):
    ngrp = s_len // tq

    def group(g):
        q0 = pl.multiple_of(g * tq, tq)
        start = pl.multiple_of(jnp.clip(g * tq - (NA_ROWS_MAX // 2) * GRID_W, 0, s_len - tk), GRID_W)
        var = jnp.where(g > 0, 1, 0) + jnp.where(g == ngrp - 1, 1, 0)
        kwin = k_ref[pl.ds(start, tk), :]
        vwin = _with_ones(v_ref[pl.ds(start, tk), :])
        for qs in range(tq // LANES):
            per_head = []
            for hh in range(2):
                rows = slice(hh * tq + qs * LANES, hh * tq + (qs + 1) * LANES)
                q = _mask_head_lanes(q_ref[pl.ds(q0 + qs * LANES, LANES), :], hh)
                o, _ = _softmax_pv(_nt_dot(q, kwin) + bm_ref[var, rows, :], vwin)
                per_head.append(o)
            o_ref[pl.ds(q0 + qs * LANES, LANES), :] = _pick_head_lanes(*per_head).astype(BF16)

    def body(u, carry):
        for k in range(unroll):
            group(u * unroll + k)
        return carry

    lax.fori_loop(0, ngrp // unroll, body, 0)


def _attn_b(nat, bm, *, qrows):
    b, s, _ = nat.shape
    tq = qrows * GRID_W
    tk = (qrows + NA_ROWS_MAX) * GRID_W
    ngrp = s // tq
    npair = B_HEADS // 2

    def seq(name):
        return pl.BlockSpec((None, s, PAIR_W), lambda pr, bi: (bi, 0, NAT_BLK[name] + pr))

    return pl.pallas_call(
        functools.partial(_attn_b_kernel, tq=tq, tk=tk, s_len=s, unroll=math.gcd(B_UNROLL, ngrp)),
        grid=(npair, b),
        in_specs=[seq("bq"), seq("bk"), seq("bv"),
                  pl.BlockSpec((None, 3, 2 * tq, tk), lambda pr, bi: (pr, 0, 0, 0))],
        out_specs=pl.BlockSpec((None, s, PAIR_W), lambda pr, bi: (bi, 0, pr)),
        out_shape=jax.ShapeDtypeStruct((b, s, B_W), BF16),
        compiler_params=_params("parallel", "parallel"),
        name="attn_nbr",
    )(nat, nat, nat, bm)


def _b_bias_tiles(rpb, qrows, rows):
    kh = min(NA_ROWS_MAX, rows)
    kw = min(NA_COLS, GRID_W)
    krows = qrows + NA_ROWS_MAX
    ngrp = rows // qrows
    n_rel_r, n_rel_c = 2 * NA_ROWS_MAX - 1, 2 * NA_COLS - 1
    qc = np.arange(GRID_W)[:, None]
    kc = np.arange(GRID_W)[None, :]
    cstart = np.clip(qc - kw // 2, 0, GRID_W - kw)
    col_ok = (kc >= cstart) & (kc < cstart + kw)
    rel_c = kc - qc + (NA_COLS - 1)
    onehot_c = (rel_c[None] == np.arange(n_rel_c)[:, None, None]).astype(np.float32)
    blocks = jnp.einsum("hac,cqk->haqk", rpb.astype(F32), onehot_c, precision=lax.Precision.HIGHEST)
    blocks = jnp.where(col_ok[None, None], blocks, MASKED)
    blocks = jnp.concatenate([blocks, jnp.full((B_HEADS, 1, GRID_W, GRID_W), MASKED, F32)], axis=1)
    block_idx = []
    for g in (0, 1, ngrp - 1):
        r_abs = g * qrows + np.arange(qrows)[:, None]
        k0 = int(np.clip(g * qrows - NA_ROWS_MAX // 2, 0, rows - krows))
        kr_abs = k0 + np.arange(krows)[None, :]
        rstart = np.clip(r_abs - kh // 2, 0, rows - kh)
        row_ok = (kr_abs >= rstart) & (kr_abs < rstart + kh)
        block_idx.append(np.where(row_ok, kr_abs - r_abs + (NA_ROWS_MAX - 1), n_rel_r))
    block_idx = np.stack(block_idx)
    tiles = jnp.take(blocks, block_idx.reshape(-1), axis=1)
    tiles = tiles.reshape(B_HEADS // 2, 2, 3, qrows, krows, GRID_W, GRID_W)
    tiles = tiles.transpose(0, 2, 1, 3, 5, 4, 6)
    return tiles.reshape(B_HEADS // 2, 3, 2 * qrows * GRID_W, krows * GRID_W)


def _attn_c_kernel(tbl_ref, q_ref, k_ref, v_ref, o_ref, lse_ref, bm_ref, *, r, tq, half, s_r, unroll):
    tk = tq + 2 * half
    nblk = s_r // tq
    pr = pl.program_id(0)

    @pl.when(pl.program_id(1) == 0)
    def _():
        qq = lax.broadcasted_iota(jnp.int32, (tq, tk), 0)
        kk = lax.broadcasted_iota(jnp.int32, (tq, tk), 1)
        for var, off in enumerate((0, -half, -2 * half)):
            jj = kk - qq + off
            valid = jnp.abs(jj) <= half
            rel = r * jnp.clip(jj, -half, half)
            for hh in range(2):
                bias = _t5_bias(rel, tbl_ref, A_HEADS + 2 * pr + hh, r * half)
                bm_ref[var, hh * tq:(hh + 1) * tq, :] = jnp.where(valid, bias, MASKED)

    def block(i, lanes):
        q0 = pl.multiple_of(i * tq, tq)
        ws = pl.multiple_of(jnp.clip(i * tq - half, 0, s_r - tk), half)
        var = jnp.where(i > 0, 1, 0) + jnp.where(i == nblk - 1, 1, 0)
        qz = _stack_masked_q(q_ref[pl.ds(q0, tq), lanes])
        sc = _nt_dot(qz, k_ref[pl.ds(ws, tk), lanes]) + bm_ref[var]
        o, lse = _softmax_pv(sc, _with_ones(v_ref[pl.ds(ws, tk), lanes]))
        o_ref[pl.ds(q0, tq), lanes] = _pick_head_lanes(o[:tq], o[tq:]).astype(BF16)
        lse_ref[pl.ds(q0, tq), lanes] = _pick_head_lanes(lse[:tq], lse[tq:])

    for rho in range(r):
        lanes = slice(rho * LANES, (rho + 1) * LANES)

        def body(u, carry, lanes=lanes):
            for k in range(unroll):
                block(u * unroll + k, lanes)
            return carry

        if nblk == unroll:
            body(0, 0)
        else:
            lax.fori_loop(0, nblk // unroll, body, 0)


def _attn_c(c_r, t5_table, *, r, tq, half):
    b, s_r, _ = c_r.shape
    npair = C_HEADS // 2

    def in_spec(name):
        return pl.BlockSpec((None, s_r, r * PAIR_W), lambda pr, bi: (bi, 0, C_BLK[name] + pr))

    out_spec = pl.BlockSpec((None, s_r, r * PAIR_W), lambda pr, bi: (bi, 0, pr))
    o, lse = pl.pallas_call(
        functools.partial(_attn_c_kernel, r=r, tq=tq, half=half, s_r=s_r,
                          unroll=math.gcd(C_UNROLL, s_r // tq)),
        grid=(npair, b),
        in_specs=[pl.BlockSpec(memory_space=pltpu.SMEM), in_spec("cq"), in_spec("ck"), in_spec("cv")],
        out_specs=[out_spec, out_spec],
        out_shape=[jax.ShapeDtypeStruct((b, s_r, r * C_W), BF16),
                   jax.ShapeDtypeStruct((b, s_r, r * C_W), F32)],
        scratch_shapes=[pltpu.VMEM((3, 2 * tq, tq + 2 * half), F32)],
        compiler_params=_params("arbitrary", "arbitrary"),
        name=f"attn_dil{r}",
    )(t5_table, c_r, c_r, c_r)
    return o.reshape(b * s_r, r * C_W), lse.reshape(b * s_r, r * C_W)


def _merge_kernel(x_ref, g_ref, oa_ref, ob_ref, *rest, dils):
    nc = len(dils)
    oc_refs, ls_refs = rest[:nc], rest[nc:2 * nc]
    wg_ref, wa_ref, wb_ref, wc_ref, wo_ref, o_ref, tok_ref = rest[2 * nc:]
    x = x_ref[...]
    tm, d = x.shape
    h = _rms(x, g_ref[...]).astype(BF16)

    def token_order(ref, r):
        if r == 1:
            return ref[...].astype(F32)
        for rho in range(r):
            for cb in range(C_W // LANES):
                lo = (cb * r + rho) * LANES
                tok_ref[cb, pl.ds(rho, tm // r, stride=r), :] = ref[:, lo:lo + LANES].astype(F32)
        return jnp.concatenate([tok_ref[cb] for cb in range(C_W // LANES)], axis=1)

    lses = [token_order(ref, r) for ref, r in zip(ls_refs, dils)]
    mx = functools.reduce(jnp.maximum, lses)
    es = [jnp.exp(ls - mx) for ls in lses]
    num = sum(e * token_order(ref, r) for e, ref, r in zip(es, oc_refs, dils))
    oc = num / sum(es)

    merged = jnp.zeros(x.shape, F32)
    for n, (br, w_ref) in enumerate(((oa_ref[...], wa_ref), (ob_ref[...], wb_ref),
                                     (oc.astype(BF16), wc_ref))):
        gate = jnp.dot(h, wg_ref[:, n * d:(n + 1) * d], preferred_element_type=F32)
        merged = merged + jax.nn.sigmoid(gate) * jnp.dot(br, w_ref[...], preferred_element_type=F32)
    o_ref[...] = x + jnp.dot(merged.astype(BF16), wo_ref[...], preferred_element_type=F32)


def _merge(x2, g, oa, ob, ocs, lses, wg, wa, wb, wc, wo, *, tm, dils):
    n, d = x2.shape
    tok = lambda w: pl.BlockSpec((tm, w), lambda i: (i, 0))
    dil = lambda r: pl.BlockSpec((tm // r, r * C_W), lambda i: (i, 0))
    full = _resident
    return pl.pallas_call(
        functools.partial(_merge_kernel, dils=dils),
        grid=(n // tm,),
        in_specs=[tok(d), pl.BlockSpec((1, d), lambda i: (0, 0)), tok(A_W), tok(B_W)]
                 + [dil(r) for r in dils] + [dil(r) for r in dils]
                 + [full(wg), full(wa), full(wb), full(wc), full(wo)],
        out_specs=tok(d),
        out_shape=jax.ShapeDtypeStruct((n, d), F32),
        scratch_shapes=[pltpu.VMEM((C_W // LANES, tm, LANES), F32)],
        compiler_params=_params("parallel"),
        name="gated_merge",
    )(x2, g.reshape(1, d), oa, ob, *ocs, *lses, wg, wa, wb, wc, wo)


def _tiles(n_tok, s, d_ff):
    return dict(
        ffn_tm=min(1024, n_tok), ffn_tf=256 if d_ff % 256 == 0 else d_ff,
        a_tq=min(512, s),
        a_tk=min(256, s),
        b_qrows=4,
        c_tq=128,
        merge_tm=min(512, n_tok),
    )


def kernel(x, g_ff1, w1_ff1, w3_ff1, w2_ff1, g_mix, w_in, lam_q1, lam_k1, lam_q2, lam_k2, subln_g,
           na_rpb, t5_table, w_br_a, w_br_b, w_br_c, w_o, g_ff2, w1_ff2, w3_ff2, w2_ff2, g_final):
    b, s, d = x.shape
    depth = w_in.shape[0]
    n_tok = b * s
    rows = s // GRID_W
    tl = _tiles(n_tok, s, w1_ff1.shape[2])
    bf = lambda w: w.astype(BF16)
    c_cfgs = [(w // (2 * r), r) for (w, r) in C_CONFIGS]
    dils = tuple(r for _, r in c_cfgs)
    t5_table = t5_table.astype(F32)

    col = lambda i0, i1: slice(512 * i0, 512 * i1)

    x2 = x.reshape(n_tok, d)
    for l in range(depth):
        x2 = _ffn(x2, g_ff1[l], bf(w1_ff1[l]), bf(w3_ff1[l]), bf(w2_ff1[l]), g_final,
                  final_norm=False, tm=tl["ffn_tm"], tf=tl["ffn_tf"])

        w = w_in[l]
        w_nat = bf(jnp.concatenate([w[:, col(0, 2)], w[:, col(3, 6)]], axis=1))
        w_avt = bf(w[:, col(2, 3)].T)
        w_c = bf(w[:, col(6, 9)])
        nat, avt, *c_rs = _proj(x2.reshape(b, s, d), g_mix[l], w_nat, w_avt, w_c,
                                tm=tl["a_tq"], tk=tl["a_tk"], dils=dils)

        lam_init = 0.8 - 0.6 * math.exp(-0.3 * l)
        lam_vecs = jnp.stack([lam_q1[l], lam_k1[l], lam_q2[l], lam_k2[l]])
        o_a = _attn_a(nat, avt, t5_table, lam_vecs, subln_g[l],
                      tq=tl["a_tq"], tk=tl["a_tk"], lam_init=lam_init)
        o_b = _attn_b(nat, _b_bias_tiles(na_rpb[l], tl["b_qrows"], rows), qrows=tl["b_qrows"])
        c_out = [_attn_c(c_r, t5_table, r=r, tq=tl["c_tq"], half=half)
                 for c_r, (half, r) in zip(c_rs, c_cfgs)]

        x2 = _merge(x2, g_mix[l], o_a.reshape(n_tok, A_W), o_b.reshape(n_tok, B_W),
                    [o for o, _ in c_out], [ls for _, ls in c_out],
                    bf(w[:, col(9, 15)]), bf(w_br_a[l]), bf(w_br_b[l]), bf(w_br_c[l]), bf(w_o[l]),
                    tm=tl["merge_tm"], dils=dils)

        x2 = _ffn(x2, g_ff2[l], bf(w1_ff2[l]), bf(w3_ff2[l]), bf(w2_ff2[l]), g_final,
                  final_norm=(l == depth - 1), tm=tl["ffn_tm"], tf=tl["ffn_tf"])
    return x2.reshape(b, s, d)
```

```python
import functools
import math

import numpy as np
import jax
import jax.numpy as jnp
from jax import lax
from jax.experimental import pallas as pl
from jax.experimental.pallas import tpu as pltpu

HEAD_DIM = 64
A_HEADS = 4
B_HEADS = 8
C_HEADS = 8
GRID_W = 64
NA_ROWS_MAX = 8
NA_COLS = 16
C_CONFIGS = ((128, 1), (512, 4), (2048, 16))
T5_BUCKETS = 32
T5_MAX_DIST = 1024
EPS = 1e-6

LANES = 128
MXU_TILE = 256
MAX_SUBLANE_STRIDE = 4
C_UNROLL = 16
B_UNROLL = 4
VMEM_LIMIT_BYTES = 56 * 1024 * 1024

MASKED = -1e30
BF16 = jnp.bfloat16
F32 = jnp.float32

PAIR_W = 2 * HEAD_DIM
A_W = A_HEADS * PAIR_W
B_W = B_HEADS * HEAD_DIM
C_W = C_HEADS * HEAD_DIM
GROUP_W = 512
LOG2E = math.log2(math.e)
QK_SCALE = HEAD_DIM ** -0.5
NAT_BLK = {"aq": 0, "ak": 4, "bq": 8, "bk": 12, "bv": 16}
NAT_SCALE = (QK_SCALE * LOG2E, 1.0, QK_SCALE, 1.0, 1.0)
NAT_COLS = len(NAT_SCALE) * GROUP_W
C_BLK = {"cq": 0, "ck": 4, "cv": 8}
C_SCALE = (QK_SCALE, 1.0, 1.0)
C_COLS = len(C_SCALE) * GROUP_W
AVT_ROWS = PAIR_W + 16


def _params(*sem):
    return pltpu.CompilerParams(dimension_semantics=sem, vmem_limit_bytes=VMEM_LIMIT_BYTES)


def _rms(x, g):
    return x * lax.rsqrt(jnp.mean(x * x, axis=-1, keepdims=True) + EPS) * g


def _t5_abs_bucket_np(n):
    nb = T5_BUCKETS // 2
    max_exact = nb // 2
    n = np.asarray(n, np.int64)
    x = np.log(np.maximum(n, 1) / max_exact) / math.log(T5_MAX_DIST / max_exact) * (nb - max_exact)
    interior = (n > max_exact) & (n < T5_MAX_DIST)
    assert np.all(np.abs(x[interior] - np.round(x[interior])) > 1e-6)
    large = np.minimum(max_exact + np.floor(x + 1e-9).astype(np.int64), nb - 1)
    return np.where(n < max_exact, n, large)


def _t5_change_points(max_n):
    b = _t5_abs_bucket_np(np.arange(max_n + 1))
    return [(0, int(b[0]))] + [(n, int(b[n])) for n in range(1, max_n + 1) if b[n] != b[n - 1]]


def _t5_saturation(s):
    return _t5_change_points(s - 1)[-1][0]


def _t5_bias(rel, tbl_ref, head, max_n):
    nb = T5_BUCKETS // 2
    cps = _t5_change_points(max_n)
    n = jnp.abs(rel)
    neg = jnp.full(rel.shape, tbl_ref[cps[-1][1], head], F32)
    pos = jnp.full(rel.shape, tbl_ref[nb + cps[-1][1], head], F32)
    for (n0, bkt), (n1, _) in reversed(list(zip(cps[:-1], cps[1:]))):
        below = n < n1
        neg = jnp.where(below, tbl_ref[bkt, head], neg)
        pos = jnp.where(below, tbl_ref[nb + bkt, head], pos)
    return jnp.where(rel > 0, pos, neg)


def _ffn_kernel(x_ref, g_ref, w1_ref, w3_ref, w2_ref, gf_ref, o_ref, acc_ref, *, final_norm, tf):
    x = x_ref[...]
    h = _rms(x, g_ref[...]).astype(BF16)
    for c in range(w1_ref.shape[1] // tf):
        cols = slice(c * tf, (c + 1) * tf)
        a = jnp.dot(h, w1_ref[:, cols], preferred_element_type=F32)
        b = jnp.dot(h, w3_ref[:, cols], preferred_element_type=F32)
        u = (a * jax.nn.sigmoid(a) * b).astype(BF16)
        part = jnp.dot(u, w2_ref[cols, :], preferred_element_type=F32)
        if c == 0:
            acc_ref[...] = part
        else:
            acc_ref[...] += part
    y = x + 0.5 * acc_ref[...]
    if final_norm:
        y = _rms(y, gf_ref[...])
    o_ref[...] = y


def _resident(a):
    return pl.BlockSpec(a.shape, lambda *_: (0,) * a.ndim, pipeline_mode=pl.Buffered(1))


def _ffn(x2, g, w1, w3, w2, g_final, *, final_norm, tm, tf):
    n, d = x2.shape
    return pl.pallas_call(
        functools.partial(_ffn_kernel, final_norm=final_norm, tf=tf),
        grid=(n // tm,),
        in_specs=[
            pl.BlockSpec((tm, d), lambda i: (i, 0)),
            pl.BlockSpec((1, d), lambda i: (0, 0)),
            _resident(w1), _resident(w3), _resident(w2),
            pl.BlockSpec((1, d), lambda i: (0, 0)),
        ],
        out_specs=pl.BlockSpec((tm, d), lambda i: (i, 0)),
        out_shape=jax.ShapeDtypeStruct((n, d), F32),
        scratch_shapes=[pltpu.VMEM((tm, d), F32)],
        compiler_params=_params("parallel"),
        name="ffn",
    )(x2, g.reshape(1, d), w1, w3, w2, g_final.reshape(1, d))


def _proj_kernel(x_ref, g_ref, wn_ref, wvt_ref, wc_ref, nat_ref, avt_ref, *rest, dils):
    c_refs, cs_ref, hop_ref = rest[:-2], rest[-2], rest[-1]
    tm = x_ref.shape[0]
    tn = GROUP_W
    h = _rms(x_ref[...], g_ref[...]).astype(BF16)
    for c, scale in enumerate(NAT_SCALE):
        res = jnp.dot(h, wn_ref[:, c * tn:(c + 1) * tn], preferred_element_type=F32)
        nat_ref[:, c * tn:(c + 1) * tn] = (res if scale == 1.0 else res * scale).astype(BF16)
    vt = lax.dot_general(wvt_ref[...], h, (((1,), (1,)), ((), ())),
                         preferred_element_type=F32).astype(BF16)
    for hd in range(A_HEADS):
        for kb in range(avt_ref.shape[0]):
            tk = avt_ref.shape[2]
            avt_ref[kb, hd * AVT_ROWS:hd * AVT_ROWS + PAIR_W, :] = (
                vt[hd * PAIR_W:(hd + 1) * PAIR_W, kb * tk:(kb + 1) * tk])
            avt_ref[kb, hd * AVT_ROWS + PAIR_W:(hd + 1) * AVT_ROWS, :] = (
                jnp.ones((AVT_ROWS - PAIR_W, tk), BF16))
    per = tn // LANES
    for c, scale in enumerate(C_SCALE):
        res = jnp.dot(h, wc_ref[:, c * tn:(c + 1) * tn], preferred_element_type=F32)
        res = res if scale == 1.0 else res * scale
        for k in range(per):
            cs_ref[c * per + k] = res[:, k * LANES:(k + 1) * LANES]
    def put(c_ref, r, rho, cb, rows):
        lo = (cb * r + rho) * LANES
        c_ref[:, lo:lo + LANES] = rows.astype(BF16)

    for cb in range(C_COLS // LANES):
        for c_ref, r in zip(c_refs, dils):
            if r == 1:
                put(c_ref, r, 0, cb, cs_ref[cb])
            elif r <= MAX_SUBLANE_STRIDE:
                for rho in range(r):
                    put(c_ref, r, rho, cb, cs_ref[cb, pl.ds(rho, tm // r, stride=r), :])
            else:
                r2 = r // MAX_SUBLANE_STRIDE
                for b in range(MAX_SUBLANE_STRIDE):
                    hop_ref[cb, b] = cs_ref[cb, pl.ds(b, tm // MAX_SUBLANE_STRIDE,
                                                      stride=MAX_SUBLANE_STRIDE), :]
                    for a in range(r2):
                        put(c_ref, r, MAX_SUBLANE_STRIDE * a + b, cb,
                            hop_ref[cb, b, pl.ds(a, tm // r, stride=r2), :])


def _proj(x3, g, w_nat, w_avt, w_c, *, tm, tk, dils):
    b, s, d = x3.shape
    nt = s // tm
    tok = lambda w: pl.BlockSpec((None, tm, w), lambda bi, i: (bi, i, 0))
    full = _resident
    out_shapes = [jax.ShapeDtypeStruct((b, s, NAT_COLS), BF16),
                  jax.ShapeDtypeStruct((b, s // tk, A_HEADS * AVT_ROWS, tk), BF16)]
    out_specs = [tok(NAT_COLS),
                 pl.BlockSpec((None, tm // tk, A_HEADS * AVT_ROWS, tk), lambda bi, i: (bi, i, 0, 0))]
    for r in dils:
        out_shapes.append(jax.ShapeDtypeStruct((b, s // r, r * C_COLS), BF16))
        out_specs.append(pl.BlockSpec((None, tm // r, r * C_COLS), lambda bi, i: (bi, i, 0)))
    return pl.pallas_call(
        functools.partial(_proj_kernel, dils=dils),
        grid=(b, nt),
        in_specs=[tok(d), pl.BlockSpec((1, d), lambda bi, i: (0, 0)), full(w_nat), full(w_avt), full(w_c)],
        out_specs=out_specs,
        out_shape=out_shapes,
        scratch_shapes=[pltpu.VMEM((C_COLS // LANES, tm, LANES), F32),
                        pltpu.VMEM((C_COLS // LANES, MAX_SUBLANE_STRIDE, tm // MAX_SUBLANE_STRIDE, LANES),
                                   F32)],
        compiler_params=_params("parallel", "parallel"),
        name="qkv_proj",
    )(x3, g.reshape(1, d), w_nat, w_avt, w_c)


def _stack_masked_q(q):
    lane = lax.broadcasted_iota(jnp.int32, q.shape, 1)
    zero = jnp.zeros_like(q)
    return jnp.concatenate([jnp.where(lane < HEAD_DIM, q, zero),
                            jnp.where(lane >= HEAD_DIM, q, zero)], axis=0)


def _nt_dot(a, b):
    return lax.dot_general(a, b, (((1,), (1,)), ((), ())), preferred_element_type=F32)


def _pick_head_lanes(lo, hi):
    lane = lax.broadcasted_iota(jnp.int32, lo.shape, 1)
    return jnp.where(lane < HEAD_DIM, lo, hi)


def _mask_head_lanes(q, hh):
    lane = lax.broadcasted_iota(jnp.int32, q.shape, 1)
    keep = lane < HEAD_DIM if hh == 0 else lane >= HEAD_DIM
    return jnp.where(keep, q, jnp.zeros_like(q))


def _with_ones(v):
    return jnp.concatenate([v, jnp.ones_like(v)], axis=1)


def _softmax_pv(sc, v_ones):
    m = jnp.max(sc, axis=-1, keepdims=True)
    p = jnp.exp(sc - m).astype(BF16)
    res = jnp.dot(p, v_ones, preferred_element_type=F32)
    l = res[:, PAIR_W:]
    return res[:, :PAIR_W] / l, m + jnp.log(l)


def _attn_a_kernel(tbl_ref, lam_ref, q_ref, k_ref, vt_ref, g_ref, o_ref,
                   qz_ref, m_ref, acc_ref, bias_ref, s0_ref, s1_ref, cm0_ref, cm1_ref,
                   p0_ref, p1_ref, al0_ref, al1_ref, *, tq, tk, sw, d_lo, d_hi, nblk, lam_init):
    h = pl.program_id(0)
    bi = pl.program_id(1)
    i = pl.program_id(2)
    nb = T5_BUCKETS // 2
    max_rel = max(tq - 1 - d_lo * tk, d_hi * tk + tk - 1)
    sat = _t5_change_points(max_rel)[-1][1]
    n_near = d_hi - d_lo + 1
    j0 = i * (tq // tk)

    @pl.when(jnp.logical_and(bi == 0, i == 0))
    def _():
        kk = lax.broadcasted_iota(jnp.int32, (tk, tq), 0)
        qq = lax.broadcasted_iota(jnp.int32, (tk, tq), 1)
        bias_ref[0] = jnp.full((tk, tq), tbl_ref[sat, h] * LOG2E, F32)
        bias_ref[n_near + 1] = jnp.full((tk, tq), tbl_ref[nb + sat, h] * LOG2E, F32)
        for d in range(d_lo, d_hi + 1):
            bias_ref[d - d_lo + 1] = _t5_bias(d * tk + kk - qq, tbl_ref, h, max_rel) * LOG2E

    qz_ref[...] = _stack_masked_q(q_ref[...])
    m_ref[...] = jnp.full_like(m_ref, -jnp.inf)
    acc_ref[...] = jnp.zeros_like(acc_ref)

    w0 = jnp.clip(j0 + d_lo, 0, nblk - n_near)
    c_left = tbl_ref[sat, h] * LOG2E
    c_right = tbl_ref[nb + sat, h] * LOG2E

    def block_of(pos):
        if pos < n_near:
            return w0 + pos, None
        j = jnp.where(pos - n_near < w0, pos - n_near, pos)
        return j, jnp.where(j < j0, c_left, c_right)

    order = [block_of(pos) for pos in range(nblk)]

    slots = ((s0_ref, cm0_ref, p0_ref, al0_ref), (s1_ref, cm1_ref, p1_ref, al1_ref))

    def scores(pos, c):
        j, shift = order[pos]
        s_ref, cm_ref, _, _ = slots[pos % 2]
        cols = slice(c * sw, (c + 1) * sw)
        kb = k_ref[pl.ds(pl.multiple_of(j * tk, tk), tk), :]
        s = _nt_dot(kb, qz_ref[cols, :])
        if shift is None:
            tcols = slice((c * sw) % tq, (c * sw) % tq + sw)
            s = s + bias_ref[jnp.clip(j - j0, d_lo - 1, d_hi + 1) - (d_lo - 1), :, tcols]
        s_ref[:, cols] = s
        cmax = jnp.max(s, axis=0, keepdims=True)
        cm_ref[:, cols] = cmax if shift is None else cmax + shift

    def softmax(pos, c):
        _, shift = order[pos]
        s_ref, cm_ref, p_ref, al_ref = slots[pos % 2]
        cols = slice(c * sw, (c + 1) * sw)
        m_prev = m_ref[:, cols]
        m_new = jnp.maximum(m_prev, cm_ref[:, cols])
        al_ref[:, cols] = jnp.exp2(m_prev - m_new)
        sub = m_new if shift is None else m_new - shift
        p_ref[:, cols] = jnp.exp2(s_ref[:, cols] - sub).astype(BF16)
        m_ref[:, cols] = m_new

    def values(pos, c):
        j, _ = order[pos]
        _, _, p_ref, al_ref = slots[pos % 2]
        cols = slice(c * sw, (c + 1) * sw)
        acc_ref[:, cols] = al_ref[:, cols] * acc_ref[:, cols] + jnp.dot(
            vt_ref[j], p_ref[:, cols], preferred_element_type=F32)

    for tau in range(nblk + 2):
        for c in range(2 * tq // sw):
            if tau < nblk:
                scores(tau, c)
            if 0 <= tau - 1 < nblk:
                softmax(tau - 1, c)
            if 0 <= tau - 2 < nblk:
                values(tau - 2, c)

    lam = (jnp.exp(jnp.sum(lam_ref[0:1, :] * lam_ref[1:2, :], axis=-1, keepdims=True))
           - jnp.exp(jnp.sum(lam_ref[2:3, :] * lam_ref[3:4, :], axis=-1, keepdims=True)) + lam_init)
    o = acc_ref[0:PAIR_W, :] / acc_ref[PAIR_W:PAIR_W + 1, :]
    o = o[:, :tq] - lam * o[:, tq:]
    o = o * lax.rsqrt(jnp.mean(o * o, axis=0, keepdims=True) + EPS) * g_ref[...] * (1.0 - lam_init)
    o_ref[...] = o.T.astype(BF16)


def _attn_a(nat, avt, t5_table, lam, subln_g, *, tq, tk, lam_init):
    b, s, _ = nat.shape
    nblk = s // tk
    sat_n = _t5_saturation(s)
    d_lo = (-sat_n - tk + 1) // tk + 1
    d_hi = -(-(sat_n + tq - 1) // tk) - 1
    assert nblk >= d_hi - d_lo + 1, "the near-diagonal window must fit in the sequence"
    return pl.pallas_call(
        functools.partial(_attn_a_kernel, tq=tq, tk=tk, sw=min(MXU_TILE, tq), d_lo=d_lo, d_hi=d_hi,
                          nblk=nblk, lam_init=lam_init),
        grid=(A_HEADS, b, s // tq),
        in_specs=[
            pl.BlockSpec(memory_space=pltpu.SMEM),
            pl.BlockSpec((4, HEAD_DIM), lambda h, bi, i: (0, 0)),
            pl.BlockSpec((None, tq, PAIR_W), lambda h, bi, i: (bi, i, NAT_BLK["aq"] + h)),
            pl.BlockSpec((None, s, PAIR_W), lambda h, bi, i: (bi, 0, NAT_BLK["ak"] + h)),
            pl.BlockSpec((None, nblk, AVT_ROWS, tk), lambda h, bi, i: (bi, 0, h, 0)),
            pl.BlockSpec((PAIR_W, 1), lambda h, bi, i: (0, 0)),
        ],
        out_specs=pl.BlockSpec((None, tq, PAIR_W), lambda h, bi, i: (bi, i, h)),
        out_shape=jax.ShapeDtypeStruct((b, s, A_W), BF16),
        scratch_shapes=[
            pltpu.VMEM((2 * tq, PAIR_W), BF16),
            pltpu.VMEM((1, 2 * tq), F32),
            pltpu.VMEM((AVT_ROWS, 2 * tq), F32),
            pltpu.VMEM((d_hi - d_lo + 3, tk, tq), F32),
        ] + [pltpu.VMEM((tk, 2 * tq), F32)] * 2 + [pltpu.VMEM((1, 2 * tq), F32)] * 2
          + [pltpu.VMEM((tk, 2 * tq), BF16)] * 2 + [pltpu.VMEM((1, 2 * tq), F32)] * 2,
        compiler_params=_params("arbitrary", "arbitrary", "arbitrary"),
        name="attn_diff",
    )(t5_table, lam, nat, nat, avt, subln_g.reshape(PAIR_W, 1))


def _attn_b_kernel(q_ref, k_ref, v_ref, bm_ref, o_ref, *, tq, tk, s_len, unroll):
    ngrp = s_len // tq

    def group(g):
        q0 = pl.multiple_of(g * tq, tq)
        start = pl.multiple_of(jnp.clip(g * tq - (NA_ROWS_MAX // 2) * GRID_W, 0, s_len - tk), GRID_W)
        var = jnp.where(g > 0, 1, 0) + jnp.where(g == ngrp - 1, 1, 0)
        kwin = k_ref[pl.ds(start, tk), :]
        vwin = _with_ones(v_ref[pl.ds(start, tk), :])
        for qs in range(tq // LANES):
            per_head = []
            for hh in range(2):
                rows = slice(hh * tq + qs * LANES, hh * tq + (qs + 1) * LANES)
                q = _mask_head_lanes(q_ref[pl.ds(q0 + qs * LANES, LANES), :], hh)
                o, _ = _softmax_pv(_nt_dot(q, kwin) + bm_ref[var, rows, :], vwin)
                per_head.append(o)
            o_ref[pl.ds(q0 + qs * LANES, LANES), :] = _pick_head_lanes(*per_head).astype(BF16)

    def body(u, carry):
        for k in range(unroll):
            group(u * unroll + k)
        return carry

    lax.fori_loop(0, ngrp // unroll, body, 0)


def _attn_b(nat, bm, *, qrows):
    b, s, _ = nat.shape
    tq = qrows * GRID_W
    tk = (qrows + NA_ROWS_MAX) * GRID_W
    ngrp = s // tq
    npair = B_HEADS // 2

    def seq(name):
        return pl.BlockSpec((None, s, PAIR_W), lambda pr, bi: (bi, 0, NAT_BLK[name] + pr))

    return pl.pallas_call(
        functools.partial(_attn_b_kernel, tq=tq, tk=tk, s_len=s, unroll=math.gcd(B_UNROLL, ngrp)),
        grid=(npair, b),
        in_specs=[seq("bq"), seq("bk"), seq("bv"),
                  pl.BlockSpec((None, 3, 2 * tq, tk), lambda pr, bi: (pr, 0, 0, 0))],
        out_specs=pl.BlockSpec((None, s, PAIR_W), lambda pr, bi: (bi, 0, pr)),
        out_shape=jax.ShapeDtypeStruct((b, s, B_W), BF16),
        compiler_params=_params("parallel", "parallel"),
        name="attn_nbr",
    )(nat, nat, nat, bm)


def _b_bias_tiles(rpb, qrows, rows):
    kh = min(NA_ROWS_MAX, rows)
    kw = min(NA_COLS, GRID_W)
    krows = qrows + NA_ROWS_MAX
    ngrp = rows // qrows
    n_rel_r, n_rel_c = 2 * NA_ROWS_MAX - 1, 2 * NA_COLS - 1
    qc = np.arange(GRID_W)[:, None]
    kc = np.arange(GRID_W)[None, :]
    cstart = np.clip(qc - kw // 2, 0, GRID_W - kw)
    col_ok = (kc >= cstart) & (kc < cstart + kw)
    rel_c = kc - qc + (NA_COLS - 1)
    onehot_c = (rel_c[None] == np.arange(n_rel_c)[:, None, None]).astype(np.float32)
    blocks = jnp.einsum("hac,cqk->haqk", rpb.astype(F32), onehot_c, precision=lax.Precision.HIGHEST)
    blocks = jnp.where(col_ok[None, None], blocks, MASKED)
    blocks = jnp.concatenate([blocks, jnp.full((B_HEADS, 1, GRID_W, GRID_W), MASKED, F32)], axis=1)
    block_idx = []
    for g in (0, 1, ngrp - 1):
        r_abs = g * qrows + np.arange(qrows)[:, None]
        k0 = int(np.clip(g * qrows - NA_ROWS_MAX // 2, 0, rows - krows))
        kr_abs = k0 + np.arange(krows)[None, :]
        rstart = np.clip(r_abs - kh // 2, 0, rows - kh)
        row_ok = (kr_abs >= rstart) & (kr_abs < rstart + kh)
        block_idx.append(np.where(row_ok, kr_abs - r_abs + (NA_ROWS_MAX - 1), n_rel_r))
    block_idx = np.stack(block_idx)
    tiles = jnp.take(blocks, block_idx.reshape(-1), axis=1)
    tiles = tiles.reshape(B_HEADS // 2, 2, 3, qrows, krows, GRID_W, GRID_W)
    tiles = tiles.transpose(0, 2, 1, 3, 5, 4, 6)
    return tiles.reshape(B_HEADS // 2, 3, 2 * qrows * GRID_W, krows * GRID_W)


def _attn_c_kernel(tbl_ref, q_ref, k_ref, v_ref, o_ref, lse_ref, bm_ref, *, r, tq, half, s_r, unroll):
    tk = tq + 2 * half
    nblk = s_r // tq
    pr = pl.program_id(0)

    @pl.when(pl.program_id(1) == 0)
    def _():
        qq = lax.broadcasted_iota(jnp.int32, (tq, tk), 0)
        kk = lax.broadcasted_iota(jnp.int32, (tq, tk), 1)
        for var, off in enumerate((0, -half, -2 * half)):
            jj = kk - qq + off
            valid = jnp.abs(jj) <= half
            rel = r * jnp.clip(jj, -half, half)
            for hh in range(2):
                bias = _t5_bias(rel, tbl_ref, A_HEADS + 2 * pr + hh, r * half)
                bm_ref[var, hh * tq:(hh + 1) * tq, :] = jnp.where(valid, bias, MASKED)

    def block(i, lanes):
        q0 = pl.multiple_of(i * tq, tq)
        ws = pl.multiple_of(jnp.clip(i * tq - half, 0, s_r - tk), half)
        var = jnp.where(i > 0, 1, 0) + jnp.where(i == nblk - 1, 1, 0)
        qz = _stack_masked_q(q_ref[pl.ds(q0, tq), lanes])
        sc = _nt_dot(qz, k_ref[pl.ds(ws, tk), lanes]) + bm_ref[var]
        o, lse = _softmax_pv(sc, _with_ones(v_ref[pl.ds(ws, tk), lanes]))
        o_ref[pl.ds(q0, tq), lanes] = _pick_head_lanes(o[:tq], o[tq:]).astype(BF16)
        lse_ref[pl.ds(q0, tq), lanes] = _pick_head_lanes(lse[:tq], lse[tq:])

    for rho in range(r):
        lanes = slice(rho * LANES, (rho + 1) * LANES)

        def body(u, carry, lanes=lanes):
            for k in range(unroll):
                block(u * unroll + k, lanes)
            return carry

        if nblk == unroll:
            body(0, 0)
        else:
            lax.fori_loop(0, nblk // unroll, body, 0)


def _attn_c(c_r, t5_table, *, r, tq, half):
    b, s_r, _ = c_r.shape
    npair = C_HEADS // 2

    def in_spec(name):
        return pl.BlockSpec((None, s_r, r * PAIR_W), lambda pr, bi: (bi, 0, C_BLK[name] + pr))

    out_spec = pl.BlockSpec((None, s_r, r * PAIR_W), lambda pr, bi: (bi, 0, pr))
    o, lse = pl.pallas_call(
        functools.partial(_attn_c_kernel, r=r, tq=tq, half=half, s_r=s_r,
                          unroll=math.gcd(C_UNROLL, s_r // tq)),
        grid=(npair, b),
        in_specs=[pl.BlockSpec(memory_space=pltpu.SMEM), in_spec("cq"), in_spec("ck"), in_spec("cv")],
        out_specs=[out_spec, out_spec],
        out_shape=[jax.ShapeDtypeStruct((b, s_r, r * C_W), BF16),
                   jax.ShapeDtypeStruct((b, s_r, r * C_W), F32)],
        scratch_shapes=[pltpu.VMEM((3, 2 * tq, tq + 2 * half), F32)],
        compiler_params=_params("arbitrary", "arbitrary"),
        name=f"attn_dil{r}",
    )(t5_table, c_r, c_r, c_r)
    return o.reshape(b * s_r, r * C_W), lse.reshape(b * s_r, r * C_W)


def _merge_kernel(x_ref, g_ref, oa_ref, ob_ref, *rest, dils):
    nc = len(dils)
    oc_refs, ls_refs = rest[:nc], rest[nc:2 * nc]
    wg_ref, wa_ref, wb_ref, wc_ref, wo_ref, o_ref, tok_ref, hop_ref = rest[2 * nc:]
    x = x_ref[...]
    tm, d = x.shape
    h = _rms(x, g_ref[...]).astype(BF16)

    def token_order(ref, r):
        if r == 1:
            return ref[...].astype(F32)

        def take(cb, rho):
            lo = (cb * r + rho) * LANES
            return ref[:, lo:lo + LANES].astype(F32)

        for cb in range(C_W // LANES):
            if r <= MAX_SUBLANE_STRIDE:
                for rho in range(r):
                    tok_ref[cb, pl.ds(rho, tm // r, stride=r), :] = take(cb, rho)
            else:
                r2 = r // MAX_SUBLANE_STRIDE
                for b in range(MAX_SUBLANE_STRIDE):
                    for a in range(r2):
                        hop_ref[cb, b, pl.ds(a, tm // r, stride=r2), :] = take(cb, MAX_SUBLANE_STRIDE * a + b)
                    tok_ref[cb, pl.ds(b, tm // MAX_SUBLANE_STRIDE, stride=MAX_SUBLANE_STRIDE), :] = (
                        hop_ref[cb, b])
        return jnp.concatenate([tok_ref[cb] for cb in range(C_W // LANES)], axis=1)

    lses = [token_order(ref, r) for ref, r in zip(ls_refs, dils)]
    mx = functools.reduce(jnp.maximum, lses)
    es = [jnp.exp(ls - mx) for ls in lses]
    num = sum(e * token_order(ref, r) for e, ref, r in zip(es, oc_refs, dils))
    oc = num / sum(es)

    merged = jnp.zeros(x.shape, F32)
    for n, (br, w_ref) in enumerate(((oa_ref[...], wa_ref), (ob_ref[...], wb_ref),
                                     (oc.astype(BF16), wc_ref))):
        gate = jnp.dot(h, wg_ref[:, n * d:(n + 1) * d], preferred_element_type=F32)
        merged = merged + jax.nn.sigmoid(gate) * jnp.dot(br, w_ref[...], preferred_element_type=F32)
    o_ref[...] = x + jnp.dot(merged.astype(BF16), wo_ref[...], preferred_element_type=F32)


def _merge(x2, g, oa, ob, ocs, lses, wg, wa, wb, wc, wo, *, tm, dils):
    n, d = x2.shape
    tok = lambda w: pl.BlockSpec((tm, w), lambda i: (i, 0))
    dil = lambda r: pl.BlockSpec((tm // r, r * C_W), lambda i: (i, 0))
    full = _resident
    return pl.pallas_call(
        functools.partial(_merge_kernel, dils=dils),
        grid=(n // tm,),
        in_specs=[tok(d), pl.BlockSpec((1, d), lambda i: (0, 0)), tok(A_W), tok(B_W)]
                 + [dil(r) for r in dils] + [dil(r) for r in dils]
                 + [full(wg), full(wa), full(wb), full(wc), full(wo)],
        out_specs=tok(d),
        out_shape=jax.ShapeDtypeStruct((n, d), F32),
        scratch_shapes=[pltpu.VMEM((C_W // LANES, tm, LANES), F32),
                        pltpu.VMEM((C_W // LANES, MAX_SUBLANE_STRIDE, tm // MAX_SUBLANE_STRIDE, LANES), F32)],
        compiler_params=_params("parallel"),
        name="gated_merge",
    )(x2, g.reshape(1, d), oa, ob, *ocs, *lses, wg, wa, wb, wc, wo)


def _tiles(n_tok, s, d_ff):
    return dict(
        ffn_tm=min(1024, n_tok), ffn_tf=256 if d_ff % 256 == 0 else d_ff,
        a_tq=min(512, s),
        a_tk=min(256, s),
        b_qrows=4,
        c_tq=128,
        merge_tm=min(512, n_tok),
    )


def kernel(x, g_ff1, w1_ff1, w3_ff1, w2_ff1, g_mix, w_in, lam_q1, lam_k1, lam_q2, lam_k2, subln_g,
           na_rpb, t5_table, w_br_a, w_br_b, w_br_c, w_o, g_ff2, w1_ff2, w3_ff2, w2_ff2, g_final):
    b, s, d = x.shape
    depth = w_in.shape[0]
    n_tok = b * s
    rows = s // GRID_W
    tl = _tiles(n_tok, s, w1_ff1.shape[2])
    bf = lambda w: w.astype(BF16)
    c_cfgs = [(w // (2 * r), r) for (w, r) in C_CONFIGS]
    dils = tuple(r for _, r in c_cfgs)
    t5_table = t5_table.astype(F32)

    col = lambda i0, i1: slice(512 * i0, 512 * i1)

    x2 = x.reshape(n_tok, d)
    for l in range(depth):
        x2 = _ffn(x2, g_ff1[l], bf(w1_ff1[l]), bf(w3_ff1[l]), bf(w2_ff1[l]), g_final,
                  final_norm=False, tm=tl["ffn_tm"], tf=tl["ffn_tf"])

        w = w_in[l]
        w_nat = bf(jnp.concatenate([w[:, col(0, 2)], w[:, col(3, 6)]], axis=1))
        w_avt = bf(w[:, col(2, 3)].T)
        w_c = bf(w[:, col(6, 9)])
        nat, avt, *c_rs = _proj(x2.reshape(b, s, d), g_mix[l], w_nat, w_avt, w_c,
                                tm=tl["a_tq"], tk=tl["a_tk"], dils=dils)

        lam_init = 0.8 - 0.6 * math.exp(-0.3 * l)
        lam_vecs = jnp.stack([lam_q1[l], lam_k1[l], lam_q2[l], lam_k2[l]])
        o_a = _attn_a(nat, avt, t5_table, lam_vecs, subln_g[l],
                      tq=tl["a_tq"], tk=tl["a_tk"], lam_init=lam_init)
        o_b = _attn_b(nat, _b_bias_tiles(na_rpb[l], tl["b_qrows"], rows), qrows=tl["b_qrows"])
        c_out = [_attn_c(c_r, t5_table, r=r, tq=tl["c_tq"], half=half)
                 for c_r, (half, r) in zip(c_rs, c_cfgs)]

        x2 = _merge(x2, g_mix[l], o_a.reshape(n_tok, A_W), o_b.reshape(n_tok, B_W),
                    [o for o, _ in c_out], [ls for _, ls in c_out],
                    bf(w[:, col(9, 15)]), bf(w_br_a[l]), bf(w_br_b[l]), bf(w_br_c[l]), bf(w_o[l]),
                    tm=tl["merge_tm"], dils=dils)

        x2 = _ffn(x2, g_ff2[l], bf(w1_ff2[l]), bf(w3_ff2[l]), bf(w2_ff2[l]), g_final,
                  final_norm=(l == depth - 1), tm=tl["ffn_tm"], tf=tl["ffn_tf"])
    return x2.reshape(b, s, d)
```

```python
import functools
import math

import numpy as np
import jax
import jax.numpy as jnp
from jax import lax
from jax.experimental import pallas as pl
from jax.experimental.pallas import tpu as pltpu

HEAD_DIM = 64
A_HEADS = 4
B_HEADS = 8
C_HEADS = 8
GRID_W = 64
NA_ROWS_MAX = 8
NA_COLS = 16
C_CONFIGS = ((128, 1), (512, 4), (2048, 16))
T5_BUCKETS = 32
T5_MAX_DIST = 1024
EPS = 1e-6

LANES = 128
MXU_TILE = 256
MAX_SUBLANE_STRIDE = 4
C_UNROLL = 16
B_UNROLL = 4
VMEM_LIMIT_BYTES = 56 * 1024 * 1024

MASKED = -1e30
BF16 = jnp.bfloat16
F32 = jnp.float32

PAIR_W = 2 * HEAD_DIM
A_W = A_HEADS * PAIR_W
B_W = B_HEADS * HEAD_DIM
C_W = C_HEADS * HEAD_DIM
GROUP_W = 512
LOG2E = math.log2(math.e)
QK_SCALE = HEAD_DIM ** -0.5
NAT_BLK = {"aq": 0, "ak": 4, "bq": 8, "bk": 12, "bv": 16}
NAT_SCALE = (QK_SCALE * LOG2E, 1.0, QK_SCALE, 1.0, 1.0)
NAT_COLS = len(NAT_SCALE) * GROUP_W
C_BLK = {"cq": 0, "ck": 4, "cv": 8}
C_SCALE = (QK_SCALE, 1.0, 1.0)
C_COLS = len(C_SCALE) * GROUP_W
AVT_ROWS = PAIR_W + 16


def _params(*sem):
    return pltpu.CompilerParams(dimension_semantics=sem, vmem_limit_bytes=VMEM_LIMIT_BYTES)


def _rms(x, g):
    return x * lax.rsqrt(jnp.mean(x * x, axis=-1, keepdims=True) + EPS) * g


def _t5_abs_bucket_np(n):
    nb = T5_BUCKETS // 2
    max_exact = nb // 2
    n = np.asarray(n, np.int64)
    x = np.log(np.maximum(n, 1) / max_exact) / math.log(T5_MAX_DIST / max_exact) * (nb - max_exact)
    interior = (n > max_exact) & (n < T5_MAX_DIST)
    assert np.all(np.abs(x[interior] - np.round(x[interior])) > 1e-6)
    large = np.minimum(max_exact + np.floor(x + 1e-9).astype(np.int64), nb - 1)
    return np.where(n < max_exact, n, large)


def _t5_change_points(max_n):
    b = _t5_abs_bucket_np(np.arange(max_n + 1))
    return [(0, int(b[0]))] + [(n, int(b[n])) for n in range(1, max_n + 1) if b[n] != b[n - 1]]


def _t5_saturation(s):
    return _t5_change_points(s - 1)[-1][0]


def _t5_bias(rel, tbl_ref, head, max_n):
    nb = T5_BUCKETS // 2
    cps = _t5_change_points(max_n)
    n = jnp.abs(rel)
    neg = jnp.full(rel.shape, tbl_ref[cps[-1][1], head], F32)
    pos = jnp.full(rel.shape, tbl_ref[nb + cps[-1][1], head], F32)
    for (n0, bkt), (n1, _) in reversed(list(zip(cps[:-1], cps[1:]))):
        below = n < n1
        neg = jnp.where(below, tbl_ref[bkt, head], neg)
        pos = jnp.where(below, tbl_ref[nb + bkt, head], pos)
    return jnp.where(rel > 0, pos, neg)


def _ffn_kernel(x_ref, g_ref, w1_ref, w3_ref, w2_ref, gf_ref, o_ref, acc_ref, *, final_norm, tf):
    x = x_ref[...]
    h = _rms(x, g_ref[...]).astype(BF16)
    for c in range(w1_ref.shape[1] // tf):
        cols = slice(c * tf, (c + 1) * tf)
        a = jnp.dot(h, w1_ref[:, cols], preferred_element_type=F32)
        b = jnp.dot(h, w3_ref[:, cols], preferred_element_type=F32)
        u = (a * jax.nn.sigmoid(a) * b).astype(BF16)
        part = jnp.dot(u, w2_ref[cols, :], preferred_element_type=F32)
        if c == 0:
            acc_ref[...] = part
        else:
            acc_ref[...] += part
    y = x + 0.5 * acc_ref[...]
    if final_norm:
        y = _rms(y, gf_ref[...])
    o_ref[...] = y


def _resident(a):
    return pl.BlockSpec(a.shape, lambda *_: (0,) * a.ndim, pipeline_mode=pl.Buffered(1))


def _ffn(x2, g, w1, w3, w2, g_final, *, final_norm, tm, tf):
    n, d = x2.shape
    return pl.pallas_call(
        functools.partial(_ffn_kernel, final_norm=final_norm, tf=tf),
        grid=(n // tm,),
        in_specs=[
            pl.BlockSpec((tm, d), lambda i: (i, 0)),
            pl.BlockSpec((1, d), lambda i: (0, 0)),
            _resident(w1), _resident(w3), _resident(w2),
            pl.BlockSpec((1, d), lambda i: (0, 0)),
        ],
        out_specs=pl.BlockSpec((tm, d), lambda i: (i, 0)),
        out_shape=jax.ShapeDtypeStruct((n, d), F32),
        scratch_shapes=[pltpu.VMEM((tm, d), F32)],
        compiler_params=_params("parallel"),
        name="ffn",
    )(x2, g.reshape(1, d), w1, w3, w2, g_final.reshape(1, d))


def _proj_kernel(x_ref, g_ref, wn_ref, wvt_ref, wc_ref, nat_ref, avt_ref, *rest, dils):
    c_refs, cs_ref, hop_ref = rest[:-2], rest[-2], rest[-1]
    tm = x_ref.shape[0]
    tn = GROUP_W
    h = _rms(x_ref[...], g_ref[...]).astype(BF16)
    for c, scale in enumerate(NAT_SCALE):
        res = jnp.dot(h, wn_ref[:, c * tn:(c + 1) * tn], preferred_element_type=F32)
        nat_ref[:, c * tn:(c + 1) * tn] = (res if scale == 1.0 else res * scale).astype(BF16)
    vt = lax.dot_general(wvt_ref[...], h, (((1,), (1,)), ((), ())),
                         preferred_element_type=F32).astype(BF16)
    for hd in range(A_HEADS):
        for kb in range(avt_ref.shape[0]):
            tk = avt_ref.shape[2]
            avt_ref[kb, hd * AVT_ROWS:hd * AVT_ROWS + PAIR_W, :] = (
                vt[hd * PAIR_W:(hd + 1) * PAIR_W, kb * tk:(kb + 1) * tk])
            avt_ref[kb, hd * AVT_ROWS + PAIR_W:(hd + 1) * AVT_ROWS, :] = (
                jnp.ones((AVT_ROWS - PAIR_W, tk), BF16))
    per = tn // LANES
    for c, scale in enumerate(C_SCALE):
        res = jnp.dot(h, wc_ref[:, c * tn:(c + 1) * tn], preferred_element_type=F32)
        res = res if scale == 1.0 else res * scale
        for k in range(per):
            cs_ref[c * per + k] = res[:, k * LANES:(k + 1) * LANES]
    def put(c_ref, r, rho, cb, rows):
        lo = (cb * r + rho) * LANES
        c_ref[:, lo:lo + LANES] = rows.astype(BF16)

    for cb in range(C_COLS // LANES):
        for c_ref, r in zip(c_refs, dils):
            if r == 1:
                put(c_ref, r, 0, cb, cs_ref[cb])
            elif r <= MAX_SUBLANE_STRIDE:
                for rho in range(r):
                    put(c_ref, r, rho, cb, cs_ref[cb, pl.ds(rho, tm // r, stride=r), :])
            else:
                r2 = r // MAX_SUBLANE_STRIDE
                for b in range(MAX_SUBLANE_STRIDE):
                    hop_ref[cb, b] = cs_ref[cb, pl.ds(b, tm // MAX_SUBLANE_STRIDE,
                                                      stride=MAX_SUBLANE_STRIDE), :]
                    for a in range(r2):
                        put(c_ref, r, MAX_SUBLANE_STRIDE * a + b, cb,
                            hop_ref[cb, b, pl.ds(a, tm // r, stride=r2), :])


def _proj(x3, g, w_nat, w_avt, w_c, *, tm, tk, dils):
    b, s, d = x3.shape
    nt = s // tm
    tok = lambda w: pl.BlockSpec((None, tm, w), lambda bi, i: (bi, i, 0))
    full = _resident
    out_shapes = [jax.ShapeDtypeStruct((b, s, NAT_COLS), BF16),
                  jax.ShapeDtypeStruct((b, s // tk, A_HEADS * AVT_ROWS, tk), BF16)]
    out_specs = [tok(NAT_COLS),
                 pl.BlockSpec((None, tm // tk, A_HEADS * AVT_ROWS, tk), lambda bi, i: (bi, i, 0, 0))]
    for r in dils:
        out_shapes.append(jax.ShapeDtypeStruct((b, s // r, r * C_COLS), BF16))
        out_specs.append(pl.BlockSpec((None, tm // r, r * C_COLS), lambda bi, i: (bi, i, 0)))
    return pl.pallas_call(
        functools.partial(_proj_kernel, dils=dils),
        grid=(b, nt),
        in_specs=[tok(d), pl.BlockSpec((1, d), lambda bi, i: (0, 0)), full(w_nat), full(w_avt), full(w_c)],
        out_specs=out_specs,
        out_shape=out_shapes,
        scratch_shapes=[pltpu.VMEM((C_COLS // LANES, tm, LANES), F32),
                        pltpu.VMEM((C_COLS // LANES, MAX_SUBLANE_STRIDE, tm // MAX_SUBLANE_STRIDE, LANES),
                                   F32)],
        compiler_params=_params("parallel", "parallel"),
        name="qkv_proj",
    )(x3, g.reshape(1, d), w_nat, w_avt, w_c)


def _stack_masked_q(q):
    lane = lax.broadcasted_iota(jnp.int32, q.shape, 1)
    zero = jnp.zeros_like(q)
    return jnp.concatenate([jnp.where(lane < HEAD_DIM, q, zero),
                            jnp.where(lane >= HEAD_DIM, q, zero)], axis=0)


def _nt_dot(a, b):
    return lax.dot_general(a, b, (((1,), (1,)), ((), ())), preferred_element_type=F32)


def _pick_head_lanes(lo, hi):
    lane = lax.broadcasted_iota(jnp.int32, lo.shape, 1)
    return jnp.where(lane < HEAD_DIM, lo, hi)


def _mask_head_lanes(q, hh):
    lane = lax.broadcasted_iota(jnp.int32, q.shape, 1)
    keep = lane < HEAD_DIM if hh == 0 else lane >= HEAD_DIM
    return jnp.where(keep, q, jnp.zeros_like(q))


def _with_ones(v):
    return jnp.concatenate([v, jnp.ones_like(v)], axis=1)


def _softmax_pv(sc, v_ones):
    m = jnp.max(sc, axis=-1, keepdims=True)
    p = jnp.exp(sc - m).astype(BF16)
    res = jnp.dot(p, v_ones, preferred_element_type=F32)
    l = res[:, PAIR_W:]
    return res[:, :PAIR_W] / l, m + jnp.log(l)


def _attn_a_kernel(tbl_ref, lam_ref, q_ref, k_ref, vt_ref, g_ref, o_ref,
                   qz_ref, m_ref, acc_ref, bias_ref, s0_ref, s1_ref, cm0_ref, cm1_ref,
                   p0_ref, p1_ref, al0_ref, al1_ref, *, tq, tk, sw, d_lo, d_hi, nblk, lam_init):
    h = pl.program_id(0)
    bi = pl.program_id(1)
    i = pl.program_id(2)
    nb = T5_BUCKETS // 2
    max_rel = max(tq - 1 - d_lo * tk, d_hi * tk + tk - 1)
    sat = _t5_change_points(max_rel)[-1][1]
    n_near = d_hi - d_lo + 1
    j0 = i * (tq // tk)

    @pl.when(jnp.logical_and(bi == 0, i == 0))
    def _():
        kk = lax.broadcasted_iota(jnp.int32, (tk, tq), 0)
        qq = lax.broadcasted_iota(jnp.int32, (tk, tq), 1)
        bias_ref[0] = jnp.full((tk, tq), tbl_ref[sat, h] * LOG2E, F32)
        bias_ref[n_near + 1] = jnp.full((tk, tq), tbl_ref[nb + sat, h] * LOG2E, F32)
        for d in range(d_lo, d_hi + 1):
            bias_ref[d - d_lo + 1] = _t5_bias(d * tk + kk - qq, tbl_ref, h, max_rel) * LOG2E

    qz_ref[...] = _stack_masked_q(q_ref[...])
    m_ref[...] = jnp.full_like(m_ref, -jnp.inf)
    acc_ref[...] = jnp.zeros_like(acc_ref)

    w0 = jnp.clip(j0 + d_lo, 0, nblk - n_near)
    c_left = tbl_ref[sat, h] * LOG2E
    c_right = tbl_ref[nb + sat, h] * LOG2E

    def block_of(pos):
        if pos < n_near:
            return w0 + pos, None
        j = jnp.where(pos - n_near < w0, pos - n_near, pos)
        return j, jnp.where(j < j0, c_left, c_right)

    order = [block_of(pos) for pos in range(nblk)]

    slots = ((s0_ref, cm0_ref, p0_ref, al0_ref), (s1_ref, cm1_ref, p1_ref, al1_ref))
    wr = jnp.minimum(i, 0)
    rd = jnp.minimum(bi, 0)

    def scores(pos, c):
        j, shift = order[pos]
        s_ref, cm_ref, _, _ = slots[pos % 2]
        cols = slice(c * sw, (c + 1) * sw)
        kb = k_ref[pl.ds(pl.multiple_of(j * tk, tk), tk), :]
        s = _nt_dot(kb, qz_ref[cols, :])
        if shift is None:
            tcols = slice((c * sw) % tq, (c * sw) % tq + sw)
            s = s + bias_ref[jnp.clip(j - j0, d_lo - 1, d_hi + 1) - (d_lo - 1), :, tcols]
        s_ref[0, :, cols] = s
        cmax = jnp.max(s, axis=0, keepdims=True)
        cm_ref[:, cols] = cmax if shift is None else cmax + shift

    def softmax(pos, c):
        _, shift = order[pos]
        s_ref, cm_ref, p_ref, al_ref = slots[pos % 2]
        cols = slice(c * sw, (c + 1) * sw)
        m_prev = m_ref[:, cols]
        m_new = jnp.maximum(m_prev, cm_ref[:, cols])
        al_ref[:, cols] = jnp.exp2(m_prev - m_new)
        sub = m_new if shift is None else m_new - shift
        p_ref[wr, :, cols] = jnp.exp2(s_ref[0, :, cols] - sub).astype(BF16)
        m_ref[:, cols] = m_new

    def values(pos, c):
        j, _ = order[pos]
        _, _, p_ref, al_ref = slots[pos % 2]
        cols = slice(c * sw, (c + 1) * sw)
        acc_ref[:, cols] = al_ref[:, cols] * acc_ref[:, cols] + jnp.dot(
            vt_ref[j], p_ref[rd, :, cols], preferred_element_type=F32)

    for tau in range(nblk + 2):
        for c in range(2 * tq // sw):
            if tau < nblk:
                scores(tau, c)
            if 0 <= tau - 1 < nblk:
                softmax(tau - 1, c)
            if 0 <= tau - 2 < nblk:
                values(tau - 2, c)

    lam = (jnp.exp(jnp.sum(lam_ref[0:1, :] * lam_ref[1:2, :], axis=-1, keepdims=True))
           - jnp.exp(jnp.sum(lam_ref[2:3, :] * lam_ref[3:4, :], axis=-1, keepdims=True)) + lam_init)
    o = acc_ref[0:PAIR_W, :] / acc_ref[PAIR_W:PAIR_W + 1, :]
    o = o[:, :tq] - lam * o[:, tq:]
    o = o * lax.rsqrt(jnp.mean(o * o, axis=0, keepdims=True) + EPS) * g_ref[...] * (1.0 - lam_init)
    o_ref[...] = o.T.astype(BF16)


def _attn_a(nat, avt, t5_table, lam, subln_g, *, tq, tk, lam_init):
    b, s, _ = nat.shape
    nblk = s // tk
    sat_n = _t5_saturation(s)
    d_lo = (-sat_n - tk + 1) // tk + 1
    d_hi = -(-(sat_n + tq - 1) // tk) - 1
    assert nblk >= d_hi - d_lo + 1, "the near-diagonal window must fit in the sequence"
    return pl.pallas_call(
        functools.partial(_attn_a_kernel, tq=tq, tk=tk, sw=min(MXU_TILE, tq), d_lo=d_lo, d_hi=d_hi,
                          nblk=nblk, lam_init=lam_init),
        grid=(A_HEADS, b, s // tq),
        in_specs=[
            pl.BlockSpec(memory_space=pltpu.SMEM),
            pl.BlockSpec((4, HEAD_DIM), lambda h, bi, i: (0, 0)),
            pl.BlockSpec((None, tq, PAIR_W), lambda h, bi, i: (bi, i, NAT_BLK["aq"] + h)),
            pl.BlockSpec((None, s, PAIR_W), lambda h, bi, i: (bi, 0, NAT_BLK["ak"] + h)),
            pl.BlockSpec((None, nblk, AVT_ROWS, tk), lambda h, bi, i: (bi, 0, h, 0)),
            pl.BlockSpec((PAIR_W, 1), lambda h, bi, i: (0, 0)),
        ],
        out_specs=pl.BlockSpec((None, tq, PAIR_W), lambda h, bi, i: (bi, i, h)),
        out_shape=jax.ShapeDtypeStruct((b, s, A_W), BF16),
        scratch_shapes=[
            pltpu.VMEM((2 * tq, PAIR_W), BF16),
            pltpu.VMEM((1, 2 * tq), F32),
            pltpu.VMEM((AVT_ROWS, 2 * tq), F32),
            pltpu.VMEM((d_hi - d_lo + 3, tk, tq), F32),
        ] + [pltpu.VMEM((1, tk, 2 * tq), F32)] * 2 + [pltpu.VMEM((1, 2 * tq), F32)] * 2
          + [pltpu.VMEM((1, tk, 2 * tq), BF16)] * 2 + [pltpu.VMEM((1, 2 * tq), F32)] * 2,
        compiler_params=_params("arbitrary", "arbitrary", "arbitrary"),
        name="attn_diff",
    )(t5_table, lam, nat, nat, avt, subln_g.reshape(PAIR_W, 1))


def _attn_b_kernel(q_ref, k_ref, v_ref, bm_ref, o_ref, *, tq, tk, s_len, unroll):
    ngrp = s_len // tq

    def group(g):
        q0 = pl.multiple_of(g * tq, tq)
        start = pl.multiple_of(jnp.clip(g * tq - (NA_ROWS_MAX // 2) * GRID_W, 0, s_len - tk), GRID_W)
        var = jnp.where(g > 0, 1, 0) + jnp.where(g == ngrp - 1, 1, 0)
        kwin = k_ref[pl.ds(start, tk), :]
        vwin = _with_ones(v_ref[pl.ds(start, tk), :])
        for qs in range(tq // LANES):
            per_head = []
            for hh in range(2):
                rows = slice(hh * tq + qs * LANES, hh * tq + (qs + 1) * LANES)
                q = _mask_head_lanes(q_ref[pl.ds(q0 + qs * LANES, LANES), :], hh)
                o, _ = _softmax_pv(_nt_dot(q, kwin) + bm_ref[var, rows, :], vwin)
                per_head.append(o)
            o_ref[pl.ds(q0 + qs * LANES, LANES), :] = _pick_head_lanes(*per_head).astype(BF16)

    def body(u, carry):
        for k in range(unroll):
            group(u * unroll + k)
        return carry

    lax.fori_loop(0, ngrp // unroll, body, 0)


def _attn_b(nat, bm, *, qrows):
    b, s, _ = nat.shape
    tq = qrows * GRID_W
    tk = (qrows + NA_ROWS_MAX) * GRID_W
    ngrp = s // tq
    npair = B_HEADS // 2

    def seq(name):
        return pl.BlockSpec((None, s, PAIR_W), lambda pr, bi: (bi, 0, NAT_BLK[name] + pr))

    return pl.pallas_call(
        functools.partial(_attn_b_kernel, tq=tq, tk=tk, s_len=s, unroll=math.gcd(B_UNROLL, ngrp)),
        grid=(npair, b),
        in_specs=[seq("bq"), seq("bk"), seq("bv"),
                  pl.BlockSpec((None, 3, 2 * tq, tk), lambda pr, bi: (pr, 0, 0, 0))],
        out_specs=pl.BlockSpec((None, s, PAIR_W), lambda pr, bi: (bi, 0, pr)),
        out_shape=jax.ShapeDtypeStruct((b, s, B_W), BF16),
        compiler_params=_params("parallel", "parallel"),
        name="attn_nbr",
    )(nat, nat, nat, bm)


def _b_bias_tiles(rpb, qrows, rows):
    kh = min(NA_ROWS_MAX, rows)
    kw = min(NA_COLS, GRID_W)
    krows = qrows + NA_ROWS_MAX
    ngrp = rows // qrows
    n_rel_r, n_rel_c = 2 * NA_ROWS_MAX - 1, 2 * NA_COLS - 1
    qc = np.arange(GRID_W)[:, None]
    kc = np.arange(GRID_W)[None, :]
    cstart = np.clip(qc - kw // 2, 0, GRID_W - kw)
    col_ok = (kc >= cstart) & (kc < cstart + kw)
    rel_c = kc - qc + (NA_COLS - 1)
    onehot_c = (rel_c[None] == np.arange(n_rel_c)[:, None, None]).astype(np.float32)
    blocks = jnp.einsum("hac,cqk->haqk", rpb.astype(F32), onehot_c, precision=lax.Precision.HIGHEST)
    blocks = jnp.where(col_ok[None, None], blocks, MASKED)
    blocks = jnp.concatenate([blocks, jnp.full((B_HEADS, 1, GRID_W, GRID_W), MASKED, F32)], axis=1)
    block_idx = []
    for g in (0, 1, ngrp - 1):
        r_abs = g * qrows + np.arange(qrows)[:, None]
        k0 = int(np.clip(g * qrows - NA_ROWS_MAX // 2, 0, rows - krows))
        kr_abs = k0 + np.arange(krows)[None, :]
        rstart = np.clip(r_abs - kh // 2, 0, rows - kh)
        row_ok = (kr_abs >= rstart) & (kr_abs < rstart + kh)
        block_idx.append(np.where(row_ok, kr_abs - r_abs + (NA_ROWS_MAX - 1), n_rel_r))
    block_idx = np.stack(block_idx)
    tiles = jnp.take(blocks, block_idx.reshape(-1), axis=1)
    tiles = tiles.reshape(B_HEADS // 2, 2, 3, qrows, krows, GRID_W, GRID_W)
    tiles = tiles.transpose(0, 2, 1, 3, 5, 4, 6)
    return tiles.reshape(B_HEADS // 2, 3, 2 * qrows * GRID_W, krows * GRID_W)


def _attn_c_kernel(tbl_ref, q_ref, k_ref, v_ref, o_ref, lse_ref, bm_ref, *, r, tq, half, s_r, unroll):
    tk = tq + 2 * half
    nblk = s_r // tq
    pr = pl.program_id(0)

    @pl.when(pl.program_id(1) == 0)
    def _():
        qq = lax.broadcasted_iota(jnp.int32, (tq, tk), 0)
        kk = lax.broadcasted_iota(jnp.int32, (tq, tk), 1)
        for var, off in enumerate((0, -half, -2 * half)):
            jj = kk - qq + off
            valid = jnp.abs(jj) <= half
            rel = r * jnp.clip(jj, -half, half)
            for hh in range(2):
                bias = _t5_bias(rel, tbl_ref, A_HEADS + 2 * pr + hh, r * half)
                bm_ref[var, hh * tq:(hh + 1) * tq, :] = jnp.where(valid, bias, MASKED)

    def block(i, lanes):
        q0 = pl.multiple_of(i * tq, tq)
        ws = pl.multiple_of(jnp.clip(i * tq - half, 0, s_r - tk), half)
        var = jnp.where(i > 0, 1, 0) + jnp.where(i == nblk - 1, 1, 0)
        qz = _stack_masked_q(q_ref[pl.ds(q0, tq), lanes])
        sc = _nt_dot(qz, k_ref[pl.ds(ws, tk), lanes]) + bm_ref[var]
        o, lse = _softmax_pv(sc, _with_ones(v_ref[pl.ds(ws, tk), lanes]))
        o_ref[pl.ds(q0, tq), lanes] = _pick_head_lanes(o[:tq], o[tq:]).astype(BF16)
        lse_ref[pl.ds(q0, tq), lanes] = _pick_head_lanes(lse[:tq], lse[tq:])

    for rho in range(r):
        lanes = slice(rho * LANES, (rho + 1) * LANES)

        def body(u, carry, lanes=lanes):
            for k in range(unroll):
                block(u * unroll + k, lanes)
            return carry

        if nblk == unroll:
            body(0, 0)
        else:
            lax.fori_loop(0, nblk // unroll, body, 0)


def _attn_c(c_r, t5_table, *, r, tq, half):
    b, s_r, _ = c_r.shape
    npair = C_HEADS // 2

    def in_spec(name):
        return pl.BlockSpec((None, s_r, r * PAIR_W), lambda pr, bi: (bi, 0, C_BLK[name] + pr))

    out_spec = pl.BlockSpec((None, s_r, r * PAIR_W), lambda pr, bi: (bi, 0, pr))
    o, lse = pl.pallas_call(
        functools.partial(_attn_c_kernel, r=r, tq=tq, half=half, s_r=s_r,
                          unroll=math.gcd(C_UNROLL, s_r // tq)),
        grid=(npair, b),
        in_specs=[pl.BlockSpec(memory_space=pltpu.SMEM), in_spec("cq"), in_spec("ck"), in_spec("cv")],
        out_specs=[out_spec, out_spec],
        out_shape=[jax.ShapeDtypeStruct((b, s_r, r * C_W), BF16),
                   jax.ShapeDtypeStruct((b, s_r, r * C_W), F32)],
        scratch_shapes=[pltpu.VMEM((3, 2 * tq, tq + 2 * half), F32)],
        compiler_params=_params("arbitrary", "arbitrary"),
        name=f"attn_dil{r}",
    )(t5_table, c_r, c_r, c_r)
    return o.reshape(b * s_r, r * C_W), lse.reshape(b * s_r, r * C_W)


def _merge_kernel(x_ref, g_ref, oa_ref, ob_ref, *rest, dils):
    nc = len(dils)
    oc_refs, ls_refs = rest[:nc], rest[nc:2 * nc]
    wg_ref, wa_ref, wb_ref, wc_ref, wo_ref, o_ref, tok_ref, hop_ref = rest[2 * nc:]
    x = x_ref[...]
    tm, d = x.shape
    h = _rms(x, g_ref[...]).astype(BF16)

    def token_order(ref, r):
        if r == 1:
            return ref[...].astype(F32)

        def take(cb, rho):
            lo = (cb * r + rho) * LANES
            return ref[:, lo:lo + LANES].astype(F32)

        for cb in range(C_W // LANES):
            if r <= MAX_SUBLANE_STRIDE:
                for rho in range(r):
                    tok_ref[cb, pl.ds(rho, tm // r, stride=r), :] = take(cb, rho)
            else:
                r2 = r // MAX_SUBLANE_STRIDE
                for b in range(MAX_SUBLANE_STRIDE):
                    for a in range(r2):
                        hop_ref[cb, b, pl.ds(a, tm // r, stride=r2), :] = take(cb, MAX_SUBLANE_STRIDE * a + b)
                    tok_ref[cb, pl.ds(b, tm // MAX_SUBLANE_STRIDE, stride=MAX_SUBLANE_STRIDE), :] = (
                        hop_ref[cb, b])
        return jnp.concatenate([tok_ref[cb] for cb in range(C_W // LANES)], axis=1)

    lses = [token_order(ref, r) for ref, r in zip(ls_refs, dils)]
    mx = functools.reduce(jnp.maximum, lses)
    es = [jnp.exp(ls - mx) for ls in lses]
    num = sum(e * token_order(ref, r) for e, ref, r in zip(es, oc_refs, dils))
    oc = num / sum(es)

    merged = jnp.zeros(x.shape, F32)
    for n, (br, w_ref) in enumerate(((oa_ref[...], wa_ref), (ob_ref[...], wb_ref),
                                     (oc.astype(BF16), wc_ref))):
        gate = jnp.dot(h, wg_ref[:, n * d:(n + 1) * d], preferred_element_type=F32)
        merged = merged + jax.nn.sigmoid(gate) * jnp.dot(br, w_ref[...], preferred_element_type=F32)
    o_ref[...] = x + jnp.dot(merged.astype(BF16), wo_ref[...], preferred_element_type=F32)


def _merge(x2, g, oa, ob, ocs, lses, wg, wa, wb, wc, wo, *, tm, dils):
    n, d = x2.shape
    tok = lambda w: pl.BlockSpec((tm, w), lambda i: (i, 0))
    dil = lambda r: pl.BlockSpec((tm // r, r * C_W), lambda i: (i, 0))
    full = _resident
    return pl.pallas_call(
        functools.partial(_merge_kernel, dils=dils),
        grid=(n // tm,),
        in_specs=[tok(d), pl.BlockSpec((1, d), lambda i: (0, 0)), tok(A_W), tok(B_W)]
                 + [dil(r) for r in dils] + [dil(r) for r in dils]
                 + [full(wg), full(wa), full(wb), full(wc), full(wo)],
        out_specs=tok(d),
        out_shape=jax.ShapeDtypeStruct((n, d), F32),
        scratch_shapes=[pltpu.VMEM((C_W // LANES, tm, LANES), F32),
                        pltpu.VMEM((C_W // LANES, MAX_SUBLANE_STRIDE, tm // MAX_SUBLANE_STRIDE, LANES), F32)],
        compiler_params=_params("parallel"),
        name="gated_merge",
    )(x2, g.reshape(1, d), oa, ob, *ocs, *lses, wg, wa, wb, wc, wo)


def _tiles(n_tok, s, d_ff):
    return dict(
        ffn_tm=min(1024, n_tok), ffn_tf=256 if d_ff % 256 == 0 else d_ff,
        a_tq=min(512, s),
        a_tk=min(256, s),
        b_qrows=4,
        c_tq=128,
        merge_tm=min(512, n_tok),
    )


def kernel(x, g_ff1, w1_ff1, w3_ff1, w2_ff1, g_mix, w_in, lam_q1, lam_k1, lam_q2, lam_k2, subln_g,
           na_rpb, t5_table, w_br_a, w_br_b, w_br_c, w_o, g_ff2, w1_ff2, w3_ff2, w2_ff2, g_final):
    b, s, d = x.shape
    depth = w_in.shape[0]
    n_tok = b * s
    rows = s // GRID_W
    tl = _tiles(n_tok, s, w1_ff1.shape[2])
    bf = lambda w: w.astype(BF16)
    c_cfgs = [(w // (2 * r), r) for (w, r) in C_CONFIGS]
    dils = tuple(r for _, r in c_cfgs)
    t5_table = t5_table.astype(F32)

    col = lambda i0, i1: slice(512 * i0, 512 * i1)

    x2 = x.reshape(n_tok, d)
    for l in range(depth):
        x2 = _ffn(x2, g_ff1[l], bf(w1_ff1[l]), bf(w3_ff1[l]), bf(w2_ff1[l]), g_final,
                  final_norm=False, tm=tl["ffn_tm"], tf=tl["ffn_tf"])

        w = w_in[l]
        w_nat = bf(jnp.concatenate([w[:, col(0, 2)], w[:, col(3, 6)]], axis=1))
        w_avt = bf(w[:, col(2, 3)].T)
        w_c = bf(w[:, col(6, 9)])
        nat, avt, *c_rs = _proj(x2.reshape(b, s, d), g_mix[l], w_nat, w_avt, w_c,
                                tm=tl["a_tq"], tk=tl["a_tk"], dils=dils)

        lam_init = 0.8 - 0.6 * math.exp(-0.3 * l)
        lam_vecs = jnp.stack([lam_q1[l], lam_k1[l], lam_q2[l], lam_k2[l]])
        o_a = _attn_a(nat, avt, t5_table, lam_vecs, subln_g[l],
                      tq=tl["a_tq"], tk=tl["a_tk"], lam_init=lam_init)
        o_b = _attn_b(nat, _b_bias_tiles(na_rpb[l], tl["b_qrows"], rows), qrows=tl["b_qrows"])
        c_out = [_attn_c(c_r, t5_table, r=r, tq=tl["c_tq"], half=half)
                 for c_r, (half, r) in zip(c_rs, c_cfgs)]

        x2 = _merge(x2, g_mix[l], o_a.reshape(n_tok, A_W), o_b.reshape(n_tok, B_W),
                    [o for o, _ in c_out], [ls for _, ls in c_out],
                    bf(w[:, col(9, 15)]), bf(w_br_a[l]), bf(w_br_b[l]), bf(w_br_c[l]), bf(w_o[l]),
                    tm=tl["merge_tm"], dils=dils)

        x2 = _ffn(x2, g_ff2[l], bf(w1_ff2[l]), bf(w3_ff2[l]), bf(w2_ff2[l]), g_final,
                  final_norm=(l == depth - 1), tm=tl["ffn_tm"], tf=tl["ffn_tf"])
    return x2.reshape(b, s, d)
```

```python
import functools
import math

import numpy as np
import jax
import jax.numpy as jnp
from jax import lax
from jax.experimental import pallas as pl
from jax.experimental.pallas import tpu as pltpu

HEAD_DIM = 64
A_HEADS = 4
B_HEADS = 8
C_HEADS = 8
GRID_W = 64
NA_ROWS_MAX = 8
NA_COLS = 16
C_CONFIGS = ((128, 1), (512, 4), (2048, 16))
T5_BUCKETS = 32
T5_MAX_DIST = 1024
EPS = 1e-6

LANES = 128
MXU_TILE = 256
MAX_SUBLANE_STRIDE = 4
C_UNROLL = 16
B_UNROLL = 8
VMEM_LIMIT_BYTES = 56 * 1024 * 1024

MASKED = -1e30
BF16 = jnp.bfloat16
F32 = jnp.float32

PAIR_W = 2 * HEAD_DIM
A_W = A_HEADS * PAIR_W
B_W = B_HEADS * HEAD_DIM
C_W = C_HEADS * HEAD_DIM
GROUP_W = 512
LOG2E = math.log2(math.e)
QK_SCALE = HEAD_DIM ** -0.5
NAT_BLK = {"aq": 0, "ak": 4, "bq": 8, "bk": 12, "bv": 16}
NAT_SCALE = (QK_SCALE * LOG2E, 1.0, QK_SCALE, 1.0, 1.0)
NAT_COLS = len(NAT_SCALE) * GROUP_W
C_BLK = {"cq": 0, "ck": 4, "cv": 8}
C_SCALE = (QK_SCALE, 1.0, 1.0)
C_COLS = len(C_SCALE) * GROUP_W
AVT_ROWS = PAIR_W + 16


def _params(*sem):
    return pltpu.CompilerParams(dimension_semantics=sem, vmem_limit_bytes=VMEM_LIMIT_BYTES)


def _rms(x, g):
    return x * lax.rsqrt(jnp.mean(x * x, axis=-1, keepdims=True) + EPS) * g


def _t5_abs_bucket_np(n):
    nb = T5_BUCKETS // 2
    max_exact = nb // 2
    n = np.asarray(n, np.int64)
    x = np.log(np.maximum(n, 1) / max_exact) / math.log(T5_MAX_DIST / max_exact) * (nb - max_exact)
    interior = (n > max_exact) & (n < T5_MAX_DIST)
    assert np.all(np.abs(x[interior] - np.round(x[interior])) > 1e-5)
    large = np.minimum(max_exact + np.floor(x + 1e-9).astype(np.int64), nb - 1)
    return np.where(n < max_exact, n, large)


def _t5_change_points(max_n):
    b = _t5_abs_bucket_np(np.arange(max_n + 1))
    return [(0, int(b[0]))] + [(n, int(b[n])) for n in range(1, max_n + 1) if b[n] != b[n - 1]]


def _t5_saturation(s):
    return _t5_change_points(s - 1)[-1][0]


def _t5_bias(rel, tbl_ref, head, max_n):
    nb = T5_BUCKETS // 2
    cps = _t5_change_points(max_n)
    n = jnp.abs(rel)
    neg = jnp.full(rel.shape, tbl_ref[cps[-1][1], head], F32)
    pos = jnp.full(rel.shape, tbl_ref[nb + cps[-1][1], head], F32)
    for (n0, bkt), (n1, _) in reversed(list(zip(cps[:-1], cps[1:]))):
        below = n < n1
        neg = jnp.where(below, tbl_ref[bkt, head], neg)
        pos = jnp.where(below, tbl_ref[nb + bkt, head], pos)
    return jnp.where(rel > 0, pos, neg)


def _ffn_kernel(x_ref, g_ref, w1_ref, w3_ref, w2_ref, gf_ref, o_ref, acc_ref, *, final_norm, tf):
    x = x_ref[...]
    h = _rms(x, g_ref[...]).astype(BF16)
    for c in range(w1_ref.shape[1] // tf):
        cols = slice(c * tf, (c + 1) * tf)
        a = jnp.dot(h, w1_ref[:, cols], preferred_element_type=F32)
        b = jnp.dot(h, w3_ref[:, cols], preferred_element_type=F32)
        u = (a * jax.nn.sigmoid(a) * b).astype(BF16)
        part = jnp.dot(u, w2_ref[cols, :], preferred_element_type=F32)
        if c == 0:
            acc_ref[...] = part
        else:
            acc_ref[...] += part
    y = x + 0.5 * acc_ref[...]
    if final_norm:
        y = _rms(y, gf_ref[...])
    o_ref[...] = y


def _resident(a):
    return pl.BlockSpec(a.shape, lambda *_: (0,) * a.ndim, pipeline_mode=pl.Buffered(1))


def _ffn(x2, g, w1, w3, w2, g_final, *, final_norm, tm, tf):
    n, d = x2.shape
    return pl.pallas_call(
        functools.partial(_ffn_kernel, final_norm=final_norm, tf=tf),
        grid=(n // tm,),
        in_specs=[
            pl.BlockSpec((tm, d), lambda i: (i, 0)),
            pl.BlockSpec((1, d), lambda i: (0, 0)),
            _resident(w1), _resident(w3), _resident(w2),
            pl.BlockSpec((1, d), lambda i: (0, 0)),
        ],
        out_specs=pl.BlockSpec((tm, d), lambda i: (i, 0)),
        out_shape=jax.ShapeDtypeStruct((n, d), F32),
        scratch_shapes=[pltpu.VMEM((tm, d), F32)],
        compiler_params=_params("parallel"),
        name="ffn",
    )(x2, g.reshape(1, d), w1, w3, w2, g_final.reshape(1, d))


def _proj_kernel(x_ref, g_ref, wn_ref, wvt_ref, wc_ref, nat_ref, avt_ref, *rest, dils):
    c_refs, cs_ref, hop_ref = rest[:-2], rest[-2], rest[-1]
    tm = x_ref.shape[0]
    tn = GROUP_W
    h = _rms(x_ref[...], g_ref[...]).astype(BF16)
    for c, scale in enumerate(NAT_SCALE):
        res = jnp.dot(h, wn_ref[:, c * tn:(c + 1) * tn], preferred_element_type=F32)
        nat_ref[:, c * tn:(c + 1) * tn] = (res if scale == 1.0 else res * scale).astype(BF16)
    vt = lax.dot_general(wvt_ref[...], h, (((1,), (1,)), ((), ())),
                         preferred_element_type=F32).astype(BF16)
    for hd in range(A_HEADS):
        for kb in range(avt_ref.shape[0]):
            tk = avt_ref.shape[2]
            avt_ref[kb, hd * AVT_ROWS:hd * AVT_ROWS + PAIR_W, :] = (
                vt[hd * PAIR_W:(hd + 1) * PAIR_W, kb * tk:(kb + 1) * tk])
            avt_ref[kb, hd * AVT_ROWS + PAIR_W:(hd + 1) * AVT_ROWS, :] = (
                jnp.ones((AVT_ROWS - PAIR_W, tk), BF16))
    per = tn // LANES
    for c, scale in enumerate(C_SCALE):
        res = jnp.dot(h, wc_ref[:, c * tn:(c + 1) * tn], preferred_element_type=F32)
        res = res if scale == 1.0 else res * scale
        for k in range(per):
            cs_ref[c * per + k] = res[:, k * LANES:(k + 1) * LANES]

    def put(c_ref, r, rho, cb, rows):
        lo = (cb * r + rho) * LANES
        c_ref[:, lo:lo + LANES] = rows.astype(BF16)

    for cb in range(C_COLS // LANES):
        for c_ref, r in zip(c_refs, dils):
            if r == 1:
                put(c_ref, r, 0, cb, cs_ref[cb])
            elif r <= MAX_SUBLANE_STRIDE:
                for rho in range(r):
                    put(c_ref, r, rho, cb, cs_ref[cb, pl.ds(rho, tm // r, stride=r), :])
            else:
                r2 = r // MAX_SUBLANE_STRIDE
                for b in range(MAX_SUBLANE_STRIDE):
                    hop_ref[cb, b] = cs_ref[cb, pl.ds(b, tm // MAX_SUBLANE_STRIDE,
                                                      stride=MAX_SUBLANE_STRIDE), :]
                    for a in range(r2):
                        put(c_ref, r, MAX_SUBLANE_STRIDE * a + b, cb,
                            hop_ref[cb, b, pl.ds(a, tm // r, stride=r2), :])


def _proj(x3, g, w_nat, w_avt, w_c, *, tm, tk, dils):
    b, s, d = x3.shape
    nt = s // tm
    tok = lambda w: pl.BlockSpec((None, tm, w), lambda bi, i: (bi, i, 0))
    full = _resident
    out_shapes = [jax.ShapeDtypeStruct((b, s, NAT_COLS), BF16),
                  jax.ShapeDtypeStruct((b, s // tk, A_HEADS * AVT_ROWS, tk), BF16)]
    out_specs = [tok(NAT_COLS),
                 pl.BlockSpec((None, tm // tk, A_HEADS * AVT_ROWS, tk), lambda bi, i: (bi, i, 0, 0))]
    for r in dils:
        out_shapes.append(jax.ShapeDtypeStruct((b, s // r, r * C_COLS), BF16))
        out_specs.append(pl.BlockSpec((None, tm // r, r * C_COLS), lambda bi, i: (bi, i, 0)))
    return pl.pallas_call(
        functools.partial(_proj_kernel, dils=dils),
        grid=(b, nt),
        in_specs=[tok(d), pl.BlockSpec((1, d), lambda bi, i: (0, 0)), full(w_nat), full(w_avt), full(w_c)],
        out_specs=out_specs,
        out_shape=out_shapes,
        scratch_shapes=[pltpu.VMEM((C_COLS // LANES, tm, LANES), F32),
                        pltpu.VMEM((C_COLS // LANES, MAX_SUBLANE_STRIDE, tm // MAX_SUBLANE_STRIDE, LANES),
                                   F32)],
        compiler_params=_params("parallel", "parallel"),
        name="qkv_proj",
    )(x3, g.reshape(1, d), w_nat, w_avt, w_c)


def _stack_masked_q(q):
    lane = lax.broadcasted_iota(jnp.int32, q.shape, 1)
    zero = jnp.zeros_like(q)
    return jnp.concatenate([jnp.where(lane < HEAD_DIM, q, zero),
                            jnp.where(lane >= HEAD_DIM, q, zero)], axis=0)


def _nt_dot(a, b):
    return lax.dot_general(a, b, (((1,), (1,)), ((), ())), preferred_element_type=F32)


def _pick_head_lanes(lo, hi):
    lane = lax.broadcasted_iota(jnp.int32, lo.shape, 1)
    return jnp.where(lane < HEAD_DIM, lo, hi)


def _mask_head_lanes(q, hh):
    lane = lax.broadcasted_iota(jnp.int32, q.shape, 1)
    keep = lane < HEAD_DIM if hh == 0 else lane >= HEAD_DIM
    return jnp.where(keep, q, jnp.zeros_like(q))


def _with_ones(v):
    return jnp.concatenate([v, jnp.ones_like(v)], axis=1)


def _softmax_pv(sc, v_ones):
    m = jnp.max(sc, axis=-1, keepdims=True)
    p = jnp.exp(sc - m).astype(BF16)
    res = jnp.dot(p, v_ones, preferred_element_type=F32)
    l = res[:, PAIR_W:]
    return res[:, :PAIR_W] / l, m + jnp.log(l)


def _attn_a_kernel(tbl_ref, lam_ref, q_ref, k_ref, vt_ref, g_ref, o_ref,
                   qz_ref, m_ref, acc_ref, bias_ref, s0_ref, s1_ref, cm0_ref, cm1_ref,
                   p0_ref, p1_ref, al0_ref, al1_ref, *, tq, tk, sw, d_lo, d_hi, nblk, lam_init):
    h = pl.program_id(0)
    bi = pl.program_id(1)
    i = pl.program_id(2)
    nb = T5_BUCKETS // 2
    max_rel = max(tq - 1 - d_lo * tk, d_hi * tk + tk - 1)
    sat = _t5_change_points(max_rel)[-1][1]
    n_near = d_hi - d_lo + 1
    j0 = i * (tq // tk)

    @pl.when(jnp.logical_and(bi == 0, i == 0))
    def _():
        kk = lax.broadcasted_iota(jnp.int32, (tk, tq), 0)
        qq = lax.broadcasted_iota(jnp.int32, (tk, tq), 1)
        bias_ref[0] = jnp.full((tk, tq), tbl_ref[sat, h] * LOG2E, F32)
        bias_ref[n_near + 1] = jnp.full((tk, tq), tbl_ref[nb + sat, h] * LOG2E, F32)
        for d in range(d_lo, d_hi + 1):
            bias_ref[d - d_lo + 1] = _t5_bias(d * tk + kk - qq, tbl_ref, h, max_rel) * LOG2E

    qz_ref[...] = _stack_masked_q(q_ref[...])
    m_ref[...] = jnp.full_like(m_ref, -jnp.inf)
    acc_ref[...] = jnp.zeros_like(acc_ref)

    w0 = jnp.clip(j0 + d_lo, 0, nblk - n_near)
    c_left = tbl_ref[sat, h] * LOG2E
    c_right = tbl_ref[nb + sat, h] * LOG2E

    def block_of(pos):
        if pos < n_near:
            return w0 + pos, None
        j = jnp.where(pos - n_near < w0, pos - n_near, pos)
        return j, jnp.where(j < j0, c_left, c_right)

    order = [block_of(pos) for pos in range(nblk)]

    slots = ((s0_ref, cm0_ref, p0_ref, al0_ref), (s1_ref, cm1_ref, p1_ref, al1_ref))

    def scores(pos, c):
        j, shift = order[pos]
        s_ref, cm_ref, _, _ = slots[pos % 2]
        cols = slice(c * sw, (c + 1) * sw)
        kb = k_ref[pl.ds(pl.multiple_of(j * tk, tk), tk), :]
        s = _nt_dot(kb, qz_ref[cols, :])
        if shift is None:
            tcols = slice((c * sw) % tq, (c * sw) % tq + sw)
            s = s + bias_ref[jnp.clip(j - j0, d_lo - 1, d_hi + 1) - (d_lo - 1), :, tcols]
        s_ref[:, cols] = s
        cmax = jnp.max(s, axis=0, keepdims=True)
        cm_ref[:, cols] = cmax if shift is None else cmax + shift

    def softmax(pos, c):
        _, shift = order[pos]
        s_ref, cm_ref, p_ref, al_ref = slots[pos % 2]
        cols = slice(c * sw, (c + 1) * sw)
        m_prev = m_ref[:, cols]
        m_new = jnp.maximum(m_prev, cm_ref[:, cols])
        al_ref[:, cols] = jnp.exp2(m_prev - m_new)
        sub = m_new if shift is None else m_new - shift
        p_ref[:, cols] = jnp.exp2(s_ref[:, cols] - sub).astype(BF16)
        m_ref[:, cols] = m_new

    def values(pos, c):
        j, _ = order[pos]
        _, _, p_ref, al_ref = slots[pos % 2]
        cols = slice(c * sw, (c + 1) * sw)
        acc_ref[:, cols] = al_ref[:, cols] * acc_ref[:, cols] + jnp.dot(
            vt_ref[j], p_ref[:, cols], preferred_element_type=F32)

    for tau in range(nblk + 2):
        for c in range(2 * tq // sw):
            if tau < nblk:
                scores(tau, c)
            if 0 <= tau - 1 < nblk:
                softmax(tau - 1, c)
            if 0 <= tau - 2 < nblk:
                values(tau - 2, c)

    lam = (jnp.exp(jnp.sum(lam_ref[0:1, :] * lam_ref[1:2, :], axis=-1, keepdims=True))
           - jnp.exp(jnp.sum(lam_ref[2:3, :] * lam_ref[3:4, :], axis=-1, keepdims=True)) + lam_init)
    o = acc_ref[0:PAIR_W, :] / acc_ref[PAIR_W:PAIR_W + 1, :]
    o = o[:, :tq] - lam * o[:, tq:]
    o = o * lax.rsqrt(jnp.mean(o * o, axis=0, keepdims=True) + EPS) * g_ref[...] * (1.0 - lam_init)
    o_ref[...] = o.T.astype(BF16)


def _attn_a(nat, avt, t5_table, lam, subln_g, *, tq, tk, lam_init):
    b, s, _ = nat.shape
    nblk = s // tk
    sat_n = _t5_saturation(s)
    d_lo = (-sat_n - tk + 1) // tk + 1
    d_hi = -(-(sat_n + tq - 1) // tk) - 1
    assert nblk >= d_hi - d_lo + 1, "the near-diagonal window must fit in the sequence"
    return pl.pallas_call(
        functools.partial(_attn_a_kernel, tq=tq, tk=tk, sw=min(MXU_TILE, tq), d_lo=d_lo, d_hi=d_hi,
                          nblk=nblk, lam_init=lam_init),
        grid=(A_HEADS, b, s // tq),
        in_specs=[
            pl.BlockSpec(memory_space=pltpu.SMEM),
            pl.BlockSpec((4, HEAD_DIM), lambda h, bi, i: (0, 0)),
            pl.BlockSpec((None, tq, PAIR_W), lambda h, bi, i: (bi, i, NAT_BLK["aq"] + h)),
            pl.BlockSpec((None, s, PAIR_W), lambda h, bi, i: (bi, 0, NAT_BLK["ak"] + h)),
            pl.BlockSpec((None, nblk, AVT_ROWS, tk), lambda h, bi, i: (bi, 0, h, 0)),
            pl.BlockSpec((PAIR_W, 1), lambda h, bi, i: (0, 0)),
        ],
        out_specs=pl.BlockSpec((None, tq, PAIR_W), lambda h, bi, i: (bi, i, h)),
        out_shape=jax.ShapeDtypeStruct((b, s, A_W), BF16),
        scratch_shapes=[
            pltpu.VMEM((2 * tq, PAIR_W), BF16),
            pltpu.VMEM((1, 2 * tq), F32),
            pltpu.VMEM((AVT_ROWS, 2 * tq), F32),
            pltpu.VMEM((d_hi - d_lo + 3, tk, tq), F32),
        ] + [pltpu.VMEM((tk, 2 * tq), F32)] * 2 + [pltpu.VMEM((1, 2 * tq), F32)] * 2
          + [pltpu.VMEM((tk, 2 * tq), BF16)] * 2 + [pltpu.VMEM((1, 2 * tq), F32)] * 2,
        compiler_params=_params("arbitrary", "arbitrary", "arbitrary"),
        name="attn_diff",
    )(t5_table, lam, nat, nat, avt, subln_g.reshape(PAIR_W, 1))


def _attn_b_kernel(q_ref, k_ref, v_ref, bm_ref, o_ref, *, tq, tk, s_len, unroll):
    ngrp = s_len // tq

    def group(g):
        q0 = pl.multiple_of(g * tq, tq)
        start = pl.multiple_of(jnp.clip(g * tq - (NA_ROWS_MAX // 2) * GRID_W, 0, s_len - tk), GRID_W)
        var = jnp.where(g > 0, 1, 0) + jnp.where(g == ngrp - 1, 1, 0)
        kwin = k_ref[pl.ds(start, tk), :]
        vwin = _with_ones(v_ref[pl.ds(start, tk), :])
        for qs in range(tq // LANES):
            per_head = []
            for hh in range(2):
                rows = slice(hh * tq + qs * LANES, hh * tq + (qs + 1) * LANES)
                q = _mask_head_lanes(q_ref[pl.ds(q0 + qs * LANES, LANES), :], hh)
                o, _ = _softmax_pv(_nt_dot(q, kwin) + bm_ref[var, rows, :], vwin)
                per_head.append(o)
            o_ref[pl.ds(q0 + qs * LANES, LANES), :] = _pick_head_lanes(*per_head).astype(BF16)

    def body(u, carry):
        for k in range(unroll):
            group(u * unroll + k)
        return carry

    lax.fori_loop(0, ngrp // unroll, body, 0)


def _attn_b(nat, bm, *, qrows):
    b, s, _ = nat.shape
    tq = qrows * GRID_W
    tk = (qrows + NA_ROWS_MAX) * GRID_W
    ngrp = s // tq
    npair = B_HEADS // 2

    def seq(name):
        return pl.BlockSpec((None, s, PAIR_W), lambda pr, bi: (bi, 0, NAT_BLK[name] + pr))

    return pl.pallas_call(
        functools.partial(_attn_b_kernel, tq=tq, tk=tk, s_len=s, unroll=math.gcd(B_UNROLL, ngrp)),
        grid=(npair, b),
        in_specs=[seq("bq"), seq("bk"), seq("bv"),
                  pl.BlockSpec((None, 3, 2 * tq, tk), lambda pr, bi: (pr, 0, 0, 0))],
        out_specs=pl.BlockSpec((None, s, PAIR_W), lambda pr, bi: (bi, 0, pr)),
        out_shape=jax.ShapeDtypeStruct((b, s, B_W), BF16),
        compiler_params=_params("parallel", "parallel"),
        name="attn_nbr",
    )(nat, nat, nat, bm)


def _b_bias_tiles(rpb, qrows, rows):
    kh = min(NA_ROWS_MAX, rows)
    kw = min(NA_COLS, GRID_W)
    krows = qrows + NA_ROWS_MAX
    ngrp = rows // qrows
    n_rel_r, n_rel_c = 2 * NA_ROWS_MAX - 1, 2 * NA_COLS - 1
    qc = np.arange(GRID_W)[:, None]
    kc = np.arange(GRID_W)[None, :]
    cstart = np.clip(qc - kw // 2, 0, GRID_W - kw)
    col_ok = (kc >= cstart) & (kc < cstart + kw)
    rel_c = kc - qc + (NA_COLS - 1)
    onehot_c = (rel_c[None] == np.arange(n_rel_c)[:, None, None]).astype(np.float32)
    blocks = jnp.einsum("hac,cqk->haqk", rpb.astype(F32), onehot_c, precision=lax.Precision.HIGHEST)
    blocks = jnp.where(col_ok[None, None], blocks, MASKED)
    blocks = jnp.concatenate([blocks, jnp.full((B_HEADS, 1, GRID_W, GRID_W), MASKED, F32)], axis=1)
    block_idx = []
    for g in (0, 1, ngrp - 1):
        r_abs = g * qrows + np.arange(qrows)[:, None]
        k0 = int(np.clip(g * qrows - NA_ROWS_MAX // 2, 0, rows - krows))
        kr_abs = k0 + np.arange(krows)[None, :]
        rstart = np.clip(r_abs - kh // 2, 0, rows - kh)
        row_ok = (kr_abs >= rstart) & (kr_abs < rstart + kh)
        block_idx.append(np.where(row_ok, kr_abs - r_abs + (NA_ROWS_MAX - 1), n_rel_r))
    block_idx = np.stack(block_idx)
    tiles = jnp.take(blocks, block_idx.reshape(-1), axis=1)
    tiles = tiles.reshape(B_HEADS // 2, 2, 3, qrows, krows, GRID_W, GRID_W)
    tiles = tiles.transpose(0, 2, 1, 3, 5, 4, 6)
    return tiles.reshape(B_HEADS // 2, 3, 2 * qrows * GRID_W, krows * GRID_W)


def _attn_c_kernel(tbl_ref, q_ref, k_ref, v_ref, o_ref, lse_ref, bm_ref, *, r, tq, half, s_r, unroll):
    tk = tq + 2 * half
    nblk = s_r // tq
    pr = pl.program_id(0)

    @pl.when(pl.program_id(1) == 0)
    def _():
        qq = lax.broadcasted_iota(jnp.int32, (tq, tk), 0)
        kk = lax.broadcasted_iota(jnp.int32, (tq, tk), 1)
        for var, off in enumerate((0, -half, -2 * half)):
            jj = kk - qq + off
            valid = jnp.abs(jj) <= half
            rel = r * jnp.clip(jj, -half, half)
            for hh in range(2):
                bias = _t5_bias(rel, tbl_ref, A_HEADS + 2 * pr + hh, r * half)
                bm_ref[var, hh * tq:(hh + 1) * tq, :] = jnp.where(valid, bias, MASKED)

    def block(i, lanes):
        q0 = pl.multiple_of(i * tq, tq)
        ws = pl.multiple_of(jnp.clip(i * tq - half, 0, s_r - tk), half)
        var = jnp.where(i > 0, 1, 0) + jnp.where(i == nblk - 1, 1, 0)
        qz = _stack_masked_q(q_ref[pl.ds(q0, tq), lanes])
        sc = _nt_dot(qz, k_ref[pl.ds(ws, tk), lanes]) + bm_ref[var]
        o, lse = _softmax_pv(sc, _with_ones(v_ref[pl.ds(ws, tk), lanes]))
        o_ref[pl.ds(q0, tq), lanes] = _pick_head_lanes(o[:tq], o[tq:]).astype(BF16)
        lse_ref[pl.ds(q0, tq), lanes] = _pick_head_lanes(lse[:tq], lse[tq:])

    for rho in range(r):
        lanes = slice(rho * LANES, (rho + 1) * LANES)

        def body(u, carry, lanes=lanes):
            for k in range(unroll):
                block(u * unroll + k, lanes)
            return carry

        if nblk == unroll:
            body(0, 0)
        else:
            lax.fori_loop(0, nblk // unroll, body, 0)


def _attn_c(c_r, t5_table, *, r, tq, half):
    b, s_r, _ = c_r.shape
    npair = C_HEADS // 2

    def in_spec(name):
        return pl.BlockSpec((None, s_r, r * PAIR_W), lambda pr, bi: (bi, 0, C_BLK[name] + pr))

    out_spec = pl.BlockSpec((None, s_r, r * PAIR_W), lambda pr, bi: (bi, 0, pr))
    o, lse = pl.pallas_call(
        functools.partial(_attn_c_kernel, r=r, tq=tq, half=half, s_r=s_r,
                          unroll=math.gcd(C_UNROLL, s_r // tq)),
        grid=(npair, b),
        in_specs=[pl.BlockSpec(memory_space=pltpu.SMEM), in_spec("cq"), in_spec("ck"), in_spec("cv")],
        out_specs=[out_spec, out_spec],
        out_shape=[jax.ShapeDtypeStruct((b, s_r, r * C_W), BF16),
                   jax.ShapeDtypeStruct((b, s_r, r * C_W), F32)],
        scratch_shapes=[pltpu.VMEM((3, 2 * tq, tq + 2 * half), F32)],
        compiler_params=_params("arbitrary", "arbitrary"),
        name=f"attn_dil{r}",
    )(t5_table, c_r, c_r, c_r)
    return o.reshape(b * s_r, r * C_W), lse.reshape(b * s_r, r * C_W)


def _merge_kernel(x_ref, g_ref, oa_ref, ob_ref, *rest, dils):
    nc = len(dils)
    oc_refs, ls_refs = rest[:nc], rest[nc:2 * nc]
    wg_ref, wa_ref, wb_ref, wc_ref, wo_ref, o_ref, tok_ref, hop_ref = rest[2 * nc:]
    x = x_ref[...]
    tm, d = x.shape
    h = _rms(x, g_ref[...]).astype(BF16)

    def token_order(ref, r):
        if r == 1:
            return ref[...].astype(F32)

        def take(cb, rho):
            lo = (cb * r + rho) * LANES
            return ref[:, lo:lo + LANES].astype(F32)

        for cb in range(C_W // LANES):
            if r <= MAX_SUBLANE_STRIDE:
                for rho in range(r):
                    tok_ref[cb, pl.ds(rho, tm // r, stride=r), :] = take(cb, rho)
            else:
                r2 = r // MAX_SUBLANE_STRIDE
                for b in range(MAX_SUBLANE_STRIDE):
                    for a in range(r2):
                        hop_ref[cb, b, pl.ds(a, tm // r, stride=r2), :] = take(cb, MAX_SUBLANE_STRIDE * a + b)
                    tok_ref[cb, pl.ds(b, tm // MAX_SUBLANE_STRIDE, stride=MAX_SUBLANE_STRIDE), :] = (
                        hop_ref[cb, b])
        return jnp.concatenate([tok_ref[cb] for cb in range(C_W // LANES)], axis=1)

    lses = [token_order(ref, r) for ref, r in zip(ls_refs, dils)]
    mx = functools.reduce(jnp.maximum, lses)
    es = [jnp.exp(ls - mx) for ls in lses]
    num = sum(e * token_order(ref, r) for e, ref, r in zip(es, oc_refs, dils))
    oc = num / sum(es)

    merged = jnp.zeros(x.shape, F32)
    for n, (br, w_ref) in enumerate(((oa_ref[...], wa_ref), (ob_ref[...], wb_ref),
                                     (oc.astype(BF16), wc_ref))):
        gate = jnp.dot(h, wg_ref[:, n * d:(n + 1) * d], preferred_element_type=F32)
        merged = merged + jax.nn.sigmoid(gate) * jnp.dot(br, w_ref[...], preferred_element_type=F32)
    o_ref[...] = x + jnp.dot(merged.astype(BF16), wo_ref[...], preferred_element_type=F32)


def _merge(x2, g, oa, ob, ocs, lses, wg, wa, wb, wc, wo, *, tm, dils):
    n, d = x2.shape
    tok = lambda w: pl.BlockSpec((tm, w), lambda i: (i, 0))
    dil = lambda r: pl.BlockSpec((tm // r, r * C_W), lambda i: (i, 0))
    full = _resident
    return pl.pallas_call(
        functools.partial(_merge_kernel, dils=dils),
        grid=(n // tm,),
        in_specs=[tok(d), pl.BlockSpec((1, d), lambda i: (0, 0)), tok(A_W), tok(B_W)]
                 + [dil(r) for r in dils] + [dil(r) for r in dils]
                 + [full(wg), full(wa), full(wb), full(wc), full(wo)],
        out_specs=tok(d),
        out_shape=jax.ShapeDtypeStruct((n, d), F32),
        scratch_shapes=[pltpu.VMEM((C_W // LANES, tm, LANES), F32),
                        pltpu.VMEM((C_W // LANES, MAX_SUBLANE_STRIDE, tm // MAX_SUBLANE_STRIDE, LANES), F32)],
        compiler_params=_params("parallel"),
        name="gated_merge",
    )(x2, g.reshape(1, d), oa, ob, *ocs, *lses, wg, wa, wb, wc, wo)


def _tiles(n_tok, s, d_ff):
    return dict(
        ffn_tm=min(1024, n_tok), ffn_tf=256 if d_ff % 256 == 0 else d_ff,
        a_tq=min(512, s),
        a_tk=min(256, s),
        b_qrows=4,
        c_tq=128,
        merge_tm=min(512, n_tok),
    )


def kernel(x, g_ff1, w1_ff1, w3_ff1, w2_ff1, g_mix, w_in, lam_q1, lam_k1, lam_q2, lam_k2, subln_g,
           na_rpb, t5_table, w_br_a, w_br_b, w_br_c, w_o, g_ff2, w1_ff2, w3_ff2, w2_ff2, g_final):
    b, s, d = x.shape
    depth = w_in.shape[0]
    n_tok = b * s
    rows = s // GRID_W
    tl = _tiles(n_tok, s, w1_ff1.shape[2])
    bf = lambda w: w.astype(BF16)
    c_cfgs = [(w // (2 * r), r) for (w, r) in C_CONFIGS]
    dils = tuple(r for _, r in c_cfgs)
    t5_table = t5_table.astype(F32)

    col = lambda i0, i1: slice(GROUP_W * i0, GROUP_W * i1)

    x2 = x.reshape(n_tok, d)
    for l in range(depth):
        x2 = _ffn(x2, g_ff1[l], bf(w1_ff1[l]), bf(w3_ff1[l]), bf(w2_ff1[l]), g_final,
                  final_norm=False, tm=tl["ffn_tm"], tf=tl["ffn_tf"])

        w = w_in[l]
        w_nat = bf(jnp.concatenate([w[:, col(0, 2)], w[:, col(3, 6)]], axis=1))
        w_avt = bf(w[:, col(2, 3)].T)
        w_c = bf(w[:, col(6, 9)])
        nat, avt, *c_rs = _proj(x2.reshape(b, s, d), g_mix[l], w_nat, w_avt, w_c,
                                tm=tl["a_tq"], tk=tl["a_tk"], dils=dils)

        lam_init = 0.8 - 0.6 * math.exp(-0.3 * l)
        lam_vecs = jnp.stack([lam_q1[l], lam_k1[l], lam_q2[l], lam_k2[l]])
        o_a = _attn_a(nat, avt, t5_table, lam_vecs, subln_g[l],
                      tq=tl["a_tq"], tk=tl["a_tk"], lam_init=lam_init)
        o_b = _attn_b(nat, _b_bias_tiles(na_rpb[l], tl["b_qrows"], rows), qrows=tl["b_qrows"])
        c_out = [_attn_c(c_r, t5_table, r=r, tq=tl["c_tq"], half=half)
                 for c_r, (half, r) in zip(c_rs, c_cfgs)]

        x2 = _merge(x2, g_mix[l], o_a.reshape(n_tok, A_W), o_b.reshape(n_tok, B_W),
                    [o for o, _ in c_out], [ls for _, ls in c_out],
                    bf(w[:, col(9, 15)]), bf(w_br_a[l]), bf(w_br_b[l]), bf(w_br_c[l]), bf(w_o[l]),
                    tm=tl["merge_tm"], dils=dils)

        x2 = _ffn(x2, g_ff2[l], bf(w1_ff2[l]), bf(w3_ff2[l]), bf(w2_ff2[l]), g_final,
                  final_norm=(l == depth - 1), tm=tl["ffn_tm"], tf=tl["ffn_tf"])
    return x2.reshape(b, s, d)
```
